```python
import math
import jax, jax.numpy as jnp
from jax import lax
import numpy as np

D_MODEL = 1024
BATCH = 4
SEQ = 8192
DEPTH = 2

RET_HEADS = 4
RET_DK = 64
RET_DV = 128
RET_THETA = 10000.0
ATT_HEADS = 4
ATT_DIM = 64
ROPE_THETA = 500000.0
ROPE_DIMS = ATT_DIM // 4
DILATED_BRANCHES = ((128, 1), (512, 4), (2048, 16))
ATT_BLOCK = 128
MLSTM_HEADS = 4
MLSTM_DIM = 64
CONV_WIDTH = 4
CHUNK = 128
MIX_WIDTH = RET_HEADS * RET_DV + ATT_HEADS * ATT_DIM + MLSTM_HEADS * MLSTM_DIM
IN_SPLITS = (RET_HEADS * RET_DK, RET_HEADS * RET_DK, RET_HEADS * RET_DV, RET_HEADS * RET_DV,
             ATT_HEADS * ATT_DIM, ATT_HEADS * ATT_DIM, ATT_HEADS * ATT_DIM,
             MLSTM_HEADS * MLSTM_DIM, MLSTM_HEADS * MLSTM_DIM, MLSTM_HEADS * MLSTM_DIM,
             MLSTM_HEADS * MLSTM_DIM, MLSTM_HEADS, MLSTM_HEADS)
W_IN_COLS = sum(IN_SPLITS)
PEER_HEADS = 8
PEER_NKEYS = 128
PEER_EXPERTS = PEER_NKEYS * PEER_NKEYS
PEER_QDIM = 256
PEER_TOPK = 16
PEER_TOKEN_BLOCK = 128
PLE_DIM = 256
DN_ALPHA = (2 * DEPTH) ** 0.25
DN_BETA = (8 * DEPTH) ** -0.25
LN_EPS = 1e-5

kernel_name = "hybrid_ret_dilattn_mlstm_peer_deepnorm"


def layer_norm(x, g, b):
    xf = x.astype(jnp.float32)
    mu = jnp.mean(xf, -1, keepdims=True)
    var = jnp.mean(jnp.square(xf - mu), -1, keepdims=True)
    return ((xf - mu) * lax.rsqrt(var + LN_EPS) * g + b).astype(x.dtype)


def head_norm(x, g):
    mu = jnp.mean(x, -1, keepdims=True)
    var = jnp.mean(jnp.square(x - mu), -1, keepdims=True)
    return (x - mu) * lax.rsqrt(var + LN_EPS) * g.astype(jnp.float32).reshape(x.shape[-2:])


def rotate(x, pos, n_rot, theta):
    half = n_rot // 2
    inv = theta ** (-jnp.arange(half, dtype=jnp.float32) / half)
    ang = pos.astype(jnp.float32)[:, None] * inv[None, :]
    cos, sin = jnp.cos(ang)[:, None, :], jnp.sin(ang)[:, None, :]
    x1, x2 = x[..., :half], x[..., half:n_rot]
    rot = jnp.concatenate([x1 * cos - x2 * sin, x2 * cos + x1 * sin], -1)
    return jnp.concatenate([rot, x[..., n_rot:]], -1)


def causal_depthwise_conv(x, w):
    C = x.shape[-1]
    return lax.conv_general_dilated(x, w[:, None, :].astype(x.dtype), window_strides=(1,),
                                    padding=[(CONV_WIDTH - 1, 0)],
                                    dimension_numbers=("NWC", "WIO", "NWC"),
                                    feature_group_count=C)


def retention(q, k, v):
    B, S, H, dk = q.shape
    nC, L = S // CHUNK, CHUNK
    lg = jnp.log(1.0 - 2.0 ** (-5.0 - jnp.arange(H, dtype=jnp.float32)))

    def chunks(t):
        return t.reshape(B, nC, L, H, -1).transpose(0, 3, 1, 2, 4)

    qc, kc, vc = chunks(q), chunks(k) * dk ** -0.5, chunks(v)
    t = jnp.arange(L, dtype=jnp.float32)
    diff = t[:, None] - t[None, :]
    decay = jnp.where(diff >= 0, jnp.exp(lg[:, None, None] * jnp.maximum(diff, 0.0)), 0.0)
    scores = jnp.einsum('bhctk,bhcsk->bhcts', qc, kc) * decay[None, :, None]
    intra = jnp.einsum('bhcts,bhcsv->bhctv', scores, vc)
    zeta = jnp.exp(lg[:, None] * (L - 1 - t)[None, :])
    xi = jnp.exp(lg[:, None] * (t + 1)[None, :])
    s_chunk = jnp.einsum('bhcsk,bhcsv->cbhkv', kc * zeta[None, :, None, :, None], vc)
    g_chunk = jnp.exp(lg * L)[None, :, None, None]

    def step(R, s):
        return g_chunk * R + s, R

    _, r_prev = lax.scan(step, jnp.zeros((B, H, dk, vc.shape[-1]), jnp.float32), s_chunk)
    cross = jnp.einsum('bhctk,cbhkv->bhctv', qc * xi[None, :, None, :, None], r_prev)
    return (intra + cross).transpose(0, 2, 3, 1, 4).reshape(B, S, H, -1)


def dilated_branch(q, k, v, window, dilation):
    B, S, H, d = q.shape
    Ld = S // dilation
    span = window // dilation
    nb = -(-Ld // ATT_BLOCK)
    Lp = nb * ATT_BLOCK

    def sub(t):
        t = t.reshape(B, Ld, dilation, H, d).transpose(0, 2, 3, 1, 4)
        return jnp.pad(t, ((0, 0), (0, 0), (0, 0), (0, Lp - Ld), (0, 0)))

    def band(t):
        tb = t.reshape(B, dilation, H, nb, ATT_BLOCK, d)
        prev = jnp.concatenate([jnp.zeros_like(tb[:, :, :, :1]), tb[:, :, :, :-1]], axis=3)
        return jnp.concatenate([prev, tb], axis=4)

    qb = sub(q).reshape(B, dilation, H, nb, ATT_BLOCK, d)
    kb, vb = band(sub(k)), band(sub(v))
    s = jnp.einsum('bxhnqd,bxhnkd->bxhnqk', qb, kb) * d ** -0.5
    iq = jnp.arange(ATT_BLOCK)
    ik = jnp.arange(2 * ATT_BLOCK)
    dist = iq[:, None] + ATT_BLOCK - ik[None, :]
    kpos = jnp.arange(nb)[:, None] * ATT_BLOCK - ATT_BLOCK + ik[None, :]
    valid = ((dist >= 0) & (dist <= span))[None] & (kpos >= 0)[:, None, :]
    s = jnp.where(valid, s, -jnp.inf)
    m = jnp.max(s, -1, keepdims=True)
    pexp = jnp.exp(s - m)
    den = jnp.sum(pexp, -1, keepdims=True)
    o = jnp.einsum('bxhnqk,bxhnkd->bxhnqd', pexp, vb) / den
    lse = (m + jnp.log(den))[..., 0]
    o = o.reshape(B, dilation, H, Lp, d)[:, :, :, :Ld].transpose(0, 3, 1, 2, 4).reshape(B, S, H, d)
    lse = lse.reshape(B, dilation, H, Lp)[:, :, :, :Ld].transpose(0, 3, 1, 2).reshape(B, S, H)
    return o, lse


def dilated_attention(q, k, v):
    outs, lses = [], []
    for window, dilation in DILATED_BRANCHES:
        o, l = dilated_branch(q, k, v, window, dilation)
        outs.append(o)
        lses.append(l)
    w = jax.nn.softmax(jnp.stack(lses, 0), axis=0)
    return jnp.einsum('rbsh,rbshd->bshd', w, jnp.stack(outs, 0))


def mlstm(q, k, v, ig, fg):
    B, S, H, d = q.shape
    nC, L = S // CHUNK, CHUNK
    k = k * d ** -0.5
    lf = jax.nn.log_sigmoid(fg)

    def to_c(t):
        return t.reshape(B, nC, L, H, -1).transpose(1, 0, 3, 2, 4)

    def gate_c(t):
        return t.reshape(B, nC, L, H).transpose(1, 0, 3, 2)

    tri = jnp.tril(jnp.ones((L, L), bool))

    def step(carry, xs):
        C, n, m = carry
        qc, kc, vc, ic, fc = xs
        b = jnp.cumsum(fc, -1)
        dm = jnp.where(tri, b[..., :, None] - b[..., None, :] + ic[..., None, :], -jnp.inf)
        inter = b + m[..., None]
        mt = jnp.maximum(inter, jnp.max(dm, -1))
        w_intra = jnp.exp(dm - mt[..., None])
        w_inter = jnp.exp(inter - mt)
        sm = jnp.einsum('bhtd,bhsd->bhts', qc, kc) * w_intra
        num = jnp.einsum('bhts,bhsd->bhtd', sm, vc) + w_inter[..., None] * jnp.einsum('bhtk,bhkv->bhtv', qc, C)
        den = jnp.sum(sm, -1) + w_inter * jnp.einsum('bhtk,bhk->bht', qc, n)
        h = num / jnp.maximum(jnp.abs(den), jnp.exp(-mt))[..., None]
        m_new = mt[..., -1]
        wk = jnp.exp(b[..., -1:] - b + ic - m_new[..., None])
        dec = jnp.exp(b[..., -1] + m - m_new)
        C_new = dec[..., None, None] * C + jnp.einsum('bhs,bhsk,bhsv->bhkv', wk, kc, vc)
        n_new = dec[..., None] * n + jnp.einsum('bhs,bhsk->bhk', wk, kc)
        return (C_new, n_new, m_new), h

    init = (jnp.zeros((B, H, d, d), jnp.float32), jnp.zeros((B, H, d), jnp.float32),
            jnp.zeros((B, H), jnp.float32))
    _, h = lax.scan(step, init, (to_c(q), to_c(k), to_c(v), gate_c(ig), gate_c(lf)))
    return h.transpose(1, 0, 3, 2, 4).reshape(B, S, H, d)


def hybrid_mixer(x, w_in, conv_w, b_if, ret_norm_g, mlstm_norm_g, w_out):
    B, S, _ = x.shape
    z = (x @ w_in).astype(jnp.float32)
    offsets = np.cumsum(IN_SPLITS)[:-1].tolist()
    rq, rk, rv, rg, aq, ak, av, mq, mk, mv, mo, mi, mf = jnp.split(z, offsets, axis=-1)
    pos = jnp.arange(S)
    rq = rotate(rq.reshape(B, S, RET_HEADS, RET_DK), pos, RET_DK, RET_THETA)
    rk = rotate(rk.reshape(B, S, RET_HEADS, RET_DK), pos, RET_DK, RET_THETA)
    ret = retention(rq, rk, rv.reshape(B, S, RET_HEADS, RET_DV))
    ret = head_norm(ret, ret_norm_g).reshape(B, S, -1) * jax.nn.silu(rg)
    aq = rotate(aq.reshape(B, S, ATT_HEADS, ATT_DIM), pos, ROPE_DIMS, ROPE_THETA)
    ak = rotate(ak.reshape(B, S, ATT_HEADS, ATT_DIM), pos, ROPE_DIMS, ROPE_THETA)
    att = dilated_attention(aq, ak, av.reshape(B, S, ATT_HEADS, ATT_DIM)).reshape(B, S, -1)
    qk = jax.nn.silu(causal_depthwise_conv(jnp.concatenate([mq, mk], -1), conv_w.astype(jnp.float32)))
    mq, mk = jnp.split(qk, 2, axis=-1)
    b_if = b_if.astype(jnp.float32)
    h = mlstm(mq.reshape(B, S, MLSTM_HEADS, MLSTM_DIM), mk.reshape(B, S, MLSTM_HEADS, MLSTM_DIM),
              mv.reshape(B, S, MLSTM_HEADS, MLSTM_DIM), mi + b_if[:MLSTM_HEADS], mf + b_if[MLSTM_HEADS:])
    mls = head_norm(h, mlstm_norm_g).reshape(B, S, -1) * jax.nn.sigmoid(mo)
    mix = jnp.concatenate([ret, att, mls], -1).astype(x.dtype)
    return mix @ w_out


def peer(x, wq, sub_keys, expert_u, expert_v):
    B, S, D = x.shape
    T = B * S
    xf = x.reshape(T, D)
    qry = (xf @ wq).reshape(T, PEER_HEADS, 2, PEER_QDIM // 2).astype(jnp.float32)
    sc = jnp.einsum('thpc,hpnc->thpn', qry, sub_keys.astype(jnp.float32))
    s1, i1 = lax.top_k(sc[:, :, 0], PEER_TOPK)
    s2, i2 = lax.top_k(sc[:, :, 1], PEER_TOPK)
    cand = (s1[..., :, None] + s2[..., None, :]).reshape(T, PEER_HEADS, PEER_TOPK * PEER_TOPK)
    cidx = (i1[..., :, None] * PEER_NKEYS + i2[..., None, :]).reshape(T, PEER_HEADS, PEER_TOPK * PEER_TOPK)
    top_s, sel = lax.top_k(cand, PEER_TOPK)
    eidx = jnp.take_along_axis(cidx, sel, axis=-1)
    g = jax.nn.softmax(top_s, axis=-1)
    nb = T // PEER_TOKEN_BLOCK
    kk = PEER_HEADS * PEER_TOPK
    xs = (xf.reshape(nb, PEER_TOKEN_BLOCK, D), eidx.reshape(nb, PEER_TOKEN_BLOCK, kk),
          g.reshape(nb, PEER_TOKEN_BLOCK, kk).astype(x.dtype))

    def block(args):
        xb, ib, gb = args
        a = jax.nn.gelu(jnp.einsum('td,tkd->tk', xb, expert_u[ib]), approximate=False)
        return jnp.einsum('tk,tkd->td', a * gb, expert_v[ib])

    return lax.map(block, xs).reshape(B, S, D)


def setup_inputs(seed: int = 0) -> dict:
    key = jax.random.key(seed)
    ks = jax.random.split(key, 24)
    nrm = lambda k, shape, s: jax.random.normal(k, shape, jnp.float32) * s
    mls_q = MLSTM_HEADS * MLSTM_DIM
    b_if = jnp.concatenate([
        nrm(ks[4], (DEPTH, MLSTM_HEADS), 0.1),
        jnp.broadcast_to(jnp.linspace(3.0, 6.0, MLSTM_HEADS), (DEPTH, MLSTM_HEADS)) + nrm(ks[5], (DEPTH, MLSTM_HEADS), 0.1),
    ], axis=-1)
    return {
        "x": nrm(ks[0], (BATCH, SEQ, D_MODEL), 1.0),
        "p": nrm(ks[1], (DEPTH, BATCH, SEQ, PLE_DIM), 1.0),
        "w_in": nrm(ks[2], (DEPTH, D_MODEL, W_IN_COLS), D_MODEL ** -0.5),
        "conv_w": nrm(ks[3], (DEPTH, CONV_WIDTH, 2 * mls_q), CONV_WIDTH ** -0.5),
        "b_if": b_if,
        "ret_norm_g": 1.0 + nrm(ks[6], (DEPTH, RET_HEADS * RET_DV), 0.02),
        "mlstm_norm_g": 1.0 + nrm(ks[7], (DEPTH, mls_q), 0.02),
        "w_out": nrm(ks[8], (DEPTH, MIX_WIDTH, D_MODEL), MIX_WIDTH ** -0.5 * DN_BETA),
        "ln1_g": 1.0 + nrm(ks[9], (DEPTH, D_MODEL), 0.02),
        "ln1_b": nrm(ks[10], (DEPTH, D_MODEL), 0.02),
        "peer_wq": nrm(ks[11], (DEPTH, D_MODEL, PEER_HEADS * PEER_QDIM), D_MODEL ** -0.5),
        "peer_keys": nrm(ks[12], (DEPTH, PEER_HEADS, 2, PEER_NKEYS, PEER_QDIM // 2), (PEER_QDIM // 2) ** -0.5),
        "peer_u": nrm(ks[13], (DEPTH, PEER_EXPERTS, D_MODEL), D_MODEL ** -0.5),
        "peer_v": nrm(ks[14], (DEPTH, PEER_EXPERTS, D_MODEL), DN_BETA),
        "ple_wg": nrm(ks[15], (DEPTH, D_MODEL, D_MODEL), D_MODEL ** -0.5),
        "ple_bg": nrm(ks[16], (DEPTH, D_MODEL), 0.02),
        "ple_wp": nrm(ks[17], (DEPTH, PLE_DIM, D_MODEL), PLE_DIM ** -0.5 * DN_BETA),
        "ln2_g": 1.0 + nrm(ks[18], (DEPTH, D_MODEL), 0.02),
        "ln2_b": nrm(ks[19], (DEPTH, D_MODEL), 0.02),
    }


def reference(x, p, w_in, conv_w, b_if, ret_norm_g, mlstm_norm_g, w_out, ln1_g, ln1_b,
              peer_wq, peer_keys, peer_u, peer_v, ple_wg, ple_bg, ple_wp, ln2_g, ln2_b):
    for i in range(DEPTH):
        mix = hybrid_mixer(x, w_in[i], conv_w[i], b_if[i], ret_norm_g[i], mlstm_norm_g[i], w_out[i])
        x1 = layer_norm(DN_ALPHA * x + mix, ln1_g[i], ln1_b[i])
        y = peer(x1, peer_wq[i], peer_keys[i], peer_u[i], peer_v[i])
        e = jax.nn.sigmoid(x1 @ ple_wg[i] + ple_bg[i]) * (p[i] @ ple_wp[i])
        x = layer_norm(DN_ALPHA * x1 + y + e, ln2_g[i], ln2_b[i])
    return x
```

```python
import functools
import math

import jax
import jax.numpy as jnp
from jax import lax
from jax.experimental import pallas as pl
from jax.experimental.pallas import tpu as pltpu

F32 = jnp.float32
BF16 = jnp.bfloat16

D_MODEL = 1024
DEPTH = 2
RET_HEADS, RET_DK, RET_DV, RET_THETA = 4, 64, 128, 10000.0
ATT_HEADS, ATT_DIM, ROPE_THETA = 4, 64, 500000.0
ROPE_DIMS = ATT_DIM // 4
DILATED_BRANCHES = ((128, 1), (512, 4), (2048, 16))
MLSTM_HEADS, MLSTM_DIM, CONV_WIDTH = 4, 64, 4
CHUNK = 128
PEER_HEADS, PEER_NKEYS, PEER_QDIM, PEER_TOPK = 8, 128, 256, 16
PEER_EXPERTS = PEER_NKEYS * PEER_NKEYS
PLE_DIM = 256
DN_ALPHA = (2 * DEPTH) ** 0.25
LN_EPS = 1e-5

Z_WIDTH = 3584
COL_RQ, COL_RK, COL_RV, COL_RG = 0, 256, 512, 1024
COL_AQ, COL_AK, COL_AV = 1536, 1792, 2048
COL_MQ, COL_MK, COL_MV, COL_MO, COL_MG = 2304, 2560, 2816, 3072, 3328

LANES = 128
SUBLANES = 8
VMEM_LIMIT = 56 * 1024 * 1024

ATT_WINDOW = 2048
ATT_KEYS = ATT_WINDOW + CHUNK

PEER_GROUP = 8
PEER_TOKENS = 64
EXPERT_ROWS = D_MODEL // LANES

NT_DIMS = (((1,), (1,)), ((), ()))
TN_DIMS = (((0,), (0,)), ((), ()))


def _params(*sem):
    return pltpu.CompilerParams(dimension_semantics=sem, vmem_limit_bytes=VMEM_LIMIT)


def _layer_norm(r, g, b):
    mu = jnp.mean(r, axis=-1, keepdims=True)
    d = r - mu
    var = jnp.mean(d * d, axis=-1, keepdims=True)
    return d * lax.rsqrt(var + LN_EPS) * g + b


def _inproj_kernel(x_ref, w_ref, z_ref):
    xb = x_ref[...].astype(BF16)
    for j in range(Z_WIDTH // 512):
        cols = slice(j * 512, (j + 1) * 512)
        z_ref[:, cols] = jnp.dot(xb, w_ref[:, cols], preferred_element_type=F32)


def _inproj(x2, w_pad):
    T = x2.shape[0]
    tm = 256
    return pl.pallas_call(
        _inproj_kernel,
        grid=(T // tm,),
        in_specs=[pl.BlockSpec((tm, D_MODEL), lambda i: (i, 0)),
                  pl.BlockSpec((D_MODEL, Z_WIDTH), lambda i: (0, 0))],
        out_specs=pl.BlockSpec((tm, Z_WIDTH), lambda i: (i, 0)),
        out_shape=jax.ShapeDtypeStruct((T, Z_WIDTH), F32),
        compiler_params=_params("parallel"),
        name="inproj",
    )(x2, w_pad)


def _prep_kernel(zr_ref, za_ref, zv_ref, cr_ref, sr_ref, ca_ref, sa_ref, rqk_ref, aqk_ref, av_ref):
    lane = lax.broadcasted_iota(jnp.int32, zr_ref.shape, 1) % 64
    width = zr_ref.shape[1]
    x = zr_ref[...]
    xs = jnp.where(lane < RET_DK // 2, pltpu.roll(x, width - RET_DK // 2, 1), pltpu.roll(x, RET_DK // 2, 1))
    rqk_ref[...] = (x * cr_ref[...] + xs * sr_ref[...]).astype(BF16)
    x = za_ref[...]
    xs = jnp.where(lane < ROPE_DIMS // 2, pltpu.roll(x, width - ROPE_DIMS // 2, 1), pltpu.roll(x, ROPE_DIMS // 2, 1))
    aqk_ref[...] = (x * ca_ref[...] + xs * sa_ref[...]).astype(BF16)
    av_ref[...] = zv_ref[...].astype(BF16)


def _prep(z, tabs, B, S):
    T = B * S
    rb = 512
    nsb = S // rb
    row = lambda s, b: (b * nsb + s, 0)
    tab = lambda s, b: (s, 0)
    return pl.pallas_call(
        _prep_kernel,
        grid=(nsb, B),
        in_specs=[pl.BlockSpec((rb, 512), lambda s, b: (b * nsb + s, COL_RQ // 512)),
                  pl.BlockSpec((rb, 512), lambda s, b: (b * nsb + s, COL_AQ // 512)),
                  pl.BlockSpec((rb, 256), lambda s, b: (b * nsb + s, COL_AV // 256)),
                  pl.BlockSpec((rb, 512), tab), pl.BlockSpec((rb, 512), tab),
                  pl.BlockSpec((rb, 512), tab), pl.BlockSpec((rb, 512), tab)],
        out_specs=[pl.BlockSpec((rb, 512), row), pl.BlockSpec((rb, 512), row), pl.BlockSpec((rb, 256), row)],
        out_shape=[jax.ShapeDtypeStruct((T, 512), BF16), jax.ShapeDtypeStruct((T, 512), BF16),
                   jax.ShapeDtypeStruct((T, 256), BF16)],
        compiler_params=_params("parallel", "parallel"),
        name="prep",
    )(z, z, z, *tabs)


def _rotary_tables(S):
    pos = jnp.arange(S, dtype=F32)

    def table(n_rot, theta, head_dim, heads):
        half = n_rot // 2
        inv = theta ** (-jnp.arange(half, dtype=F32) / half)
        ang = pos[:, None] * inv[None, :]
        cos, sin = jnp.cos(ang), jnp.sin(ang)
        rest = head_dim - n_rot
        c = jnp.concatenate([cos, cos, jnp.ones((S, rest), F32)], -1)
        s = jnp.concatenate([-sin, sin, jnp.zeros((S, rest), F32)], -1)
        return jnp.tile(c, (1, heads)), jnp.tile(s, (1, heads))

    cr, sr = table(RET_DK, RET_THETA, RET_DK, RET_HEADS)
    ca, sa = table(ROPE_DIMS, ROPE_THETA, ATT_DIM, ATT_HEADS)
    ks = RET_DK ** -0.5
    qs = ATT_DIM ** -0.5
    return (jnp.concatenate([cr, cr * ks], -1), jnp.concatenate([sr, sr * ks], -1),
            jnp.concatenate([ca * qs, ca], -1), jnp.concatenate([sa * qs, sa], -1))


def _ret_kernel(qk_ref, v_ref, g_ref, dec_ref, xi_ref, zeta_ref, gch_ref, gain_ref, o_ref, state):
    @pl.when(pl.program_id(1) == 0)
    def _():
        state[...] = jnp.zeros_like(state)

    nchunks = qk_ref.shape[0] // CHUNK
    hq = RET_HEADS * RET_DK
    for c in range(nchunks):
        rows = slice(c * CHUNK, (c + 1) * CHUNK)
        q = qk_ref[rows, 0:hq]
        k = qk_ref[rows, hq:2 * hq]
        qx = (q.astype(F32) * xi_ref[...]).astype(BF16)
        kz = (k.astype(F32) * zeta_ref[...]).astype(BF16)
        for h in range(RET_HEADS):
            dk = slice(h * RET_DK, (h + 1) * RET_DK)
            dv = slice(h * RET_DV, (h + 1) * RET_DV)
            vh = v_ref[rows, dv].astype(BF16)
            sc = lax.dot_general(q[:, dk], k[:, dk], NT_DIMS, preferred_element_type=F32) * dec_ref[h]
            r_prev = state[h]
            out = (jnp.dot(sc.astype(BF16), vh, preferred_element_type=F32)
                   + jnp.dot(qx[:, dk], r_prev.astype(BF16), preferred_element_type=F32))
            state[h] = gch_ref[h] * r_prev + lax.dot_general(kz[:, dk], vh, TN_DIMS, preferred_element_type=F32)
            mu = jnp.mean(out, axis=-1, keepdims=True)
            d = out - mu
            var = jnp.mean(d * d, axis=-1, keepdims=True)
            y = d * lax.rsqrt(var + LN_EPS) * gain_ref[:, dv]
            gate = g_ref[rows, dv]
            o_ref[rows, dv] = y * (gate * jax.nn.sigmoid(gate))


def _retention_tables():
    H, L = RET_HEADS, CHUNK
    lg = jnp.log(1.0 - 2.0 ** (-5.0 - jnp.arange(H, dtype=F32)))
    t = jnp.arange(L, dtype=F32)
    diff = t[:, None] - t[None, :]
    decay = jnp.where(diff >= 0, jnp.exp(lg[:, None, None] * jnp.maximum(diff, 0.0)), 0.0)
    zeta = jnp.exp(lg[:, None] * (L - 1 - t)[None, :])
    xi = jnp.exp(lg[:, None] * (t + 1)[None, :])
    gch = jnp.exp(lg * L)
    spread = lambda a: jnp.repeat(a.T, RET_DK, axis=1)
    return decay, spread(xi), spread(zeta), jnp.broadcast_to(gch[:, None, None], (H, RET_DK, RET_DV))


def _retention(rqk, z, gain, tabs, B, S):
    T = B * S
    rb = 512
    nsb = S // rb
    row = lambda b, s: (b * nsb + s, 0)
    const2 = lambda b, s: (0, 0)
    const3 = lambda b, s: (0, 0, 0)
    decay, xi, zeta, gch = tabs
    return pl.pallas_call(
        _ret_kernel,
        grid=(B, nsb),
        in_specs=[pl.BlockSpec((rb, 512), row),
                  pl.BlockSpec((rb, 512), lambda b, s: (b * nsb + s, COL_RV // 512)),
                  pl.BlockSpec((rb, 512), lambda b, s: (b * nsb + s, COL_RG // 512)),
                  pl.BlockSpec(decay.shape, const3), pl.BlockSpec(xi.shape, const2),
                  pl.BlockSpec(zeta.shape, const2), pl.BlockSpec(gch.shape, const3),
                  pl.BlockSpec((1, 512), const2)],
        out_specs=pl.BlockSpec((rb, 512), row),
        out_shape=jax.ShapeDtypeStruct((T, 512), F32),
        scratch_shapes=[pltpu.VMEM((RET_HEADS, RET_DK, RET_DV), F32)],
        compiler_params=_params("parallel", "arbitrary"),
        name="retention",
    )(rqk, z, z, decay, xi, zeta, gch, gain)


def _att_kernel(q_ref, k_ref, v_ref, w_ref, o_ref, kpad, vpad):
    S = q_ref.shape[0]
    kpad[0:ATT_WINDOW, :] = jnp.zeros((ATT_WINDOW, LANES), BF16)
    vpad[0:ATT_WINDOW, :] = jnp.zeros((ATT_WINDOW, LANES), BF16)
    kpad[ATT_WINDOW:ATT_WINDOW + S, :] = k_ref[...]
    vpad[ATT_WINDOW:ATT_WINDOW + S, :] = v_ref[...]
    lane = lax.broadcasted_iota(jnp.int32, (CHUNK, LANES), 1)
    col = lax.broadcasted_iota(jnp.int32, (CHUNK, ATT_KEYS), 1)

    def body(n, carry):
        base = pl.multiple_of(n * CHUNK, CHUNK)
        q = q_ref[pl.ds(base, CHUNK), :]
        kw = kpad[pl.ds(base, ATT_KEYS), :]
        vw = vpad[pl.ds(base, ATT_KEYS), :]
        wm = jnp.where(col >= ATT_WINDOW - base, w_ref[...], 0.0)
        out = jnp.zeros((CHUNK, LANES), F32)
        for h in range(2):
            head = (lane >= h * ATT_DIM) & (lane < (h + 1) * ATT_DIM)
            qh = jnp.where(head, q, jnp.zeros_like(q))
            s = lax.dot_general(qh, kw, NT_DIMS, preferred_element_type=F32)
            s = jnp.where(wm > 0.0, s, -1e30)
            m = jnp.max(s, axis=-1, keepdims=True)
            p = jnp.exp(s - m) * wm
            den = jnp.sum(p, axis=-1, keepdims=True)
            pv = jnp.dot(p.astype(BF16), vw, preferred_element_type=F32)
            out = jnp.where(head, pv / den, out)
        o_ref[pl.ds(base, CHUNK), :] = out
        return carry

    lax.fori_loop(0, S // CHUNK, body, 0)


def _attention_weights():
    r = jnp.arange(CHUNK)[:, None]
    c = jnp.arange(ATT_KEYS)[None, :]
    delta = r + ATT_WINDOW - c
    w = jnp.zeros((CHUNK, ATT_KEYS), F32)
    for window, dil in DILATED_BRANCHES:
        w = w + ((delta >= 0) & (delta <= window) & (delta % dil == 0)).astype(F32)
    return w


def _attention(aqk, av, wtab, B, S):
    T = B * S
    npair = ATT_HEADS // 2
    return pl.pallas_call(
        _att_kernel,
        grid=(B, npair),
        in_specs=[pl.BlockSpec((S, LANES), lambda b, h: (b, h)),
                  pl.BlockSpec((S, LANES), lambda b, h: (b, npair + h)),
                  pl.BlockSpec((S, LANES), lambda b, h: (b, h)),
                  pl.BlockSpec((CHUNK, ATT_KEYS), lambda b, h: (0, 0))],
        out_specs=pl.BlockSpec((S, LANES), lambda b, h: (b, h)),
        out_shape=jax.ShapeDtypeStruct((T, ATT_HEADS * ATT_DIM), F32),
        scratch_shapes=[pltpu.VMEM((ATT_WINDOW + S, LANES), BF16), pltpu.VMEM((ATT_WINDOW + S, LANES), BF16)],
        compiler_params=_params("parallel", "parallel"),
        name="dilated_attention",
    )(aqk, aqk, av, wtab)


def _mlstm_kernel(q_ref, k_ref, v_ref, o_ref, gate_ref, convw_ref, bias_ref, gain_ref, tri_ref, out_ref,
                  c_state, n_state, m_state, prev):
    HD = MLSTM_HEADS * MLSTM_DIM
    rb = q_ref.shape[0]

    @pl.when(pl.program_id(1) == 0)
    def _():
        c_state[...] = jnp.zeros_like(c_state)
        n_state[...] = jnp.zeros_like(n_state)
        m_state[...] = jnp.zeros_like(m_state)
        prev[...] = jnp.zeros_like(prev)

    cur = jnp.concatenate([q_ref[...], k_ref[...]], axis=1)
    full = jnp.concatenate([prev[...], cur], axis=0)
    prev[...] = cur[rb - SUBLANES:rb, :]
    acc = jnp.zeros_like(cur)
    for j in range(CONV_WIDTH):
        off = SUBLANES - (CONV_WIDTH - 1) + j
        acc = acc + convw_ref[j:j + 1, :] * full[off:off + rb, :]
    qk = acc * jax.nn.sigmoid(acc)
    q_all = qk[:, 0:HD]
    k_all = qk[:, HD:2 * HD] * (MLSTM_DIM ** -0.5)

    lane = lax.broadcasted_iota(jnp.int32, (CHUNK, HD), 1)
    lane1 = lax.broadcasted_iota(jnp.int32, (1, HD), 1)
    blk_r = lax.broadcasted_iota(jnp.int32, (HD, HD), 0) // MLSTM_DIM
    blk_c = lax.broadcasted_iota(jnp.int32, (HD, HD), 1) // MLSTM_DIM
    tri = tri_ref[...] > 0.0

    for c in range(rb // CHUNK):
        rows = slice(c * CHUNK, (c + 1) * CHUNK)
        q = q_all[rows]
        k = k_all[rows]
        v = v_ref[rows, :]
        qb, kb, vb = q.astype(BF16), k.astype(BF16), v.astype(BF16)
        pre = gate_ref[rows, :] + bias_ref[...]
        lf = jnp.minimum(pre, 0.0) - jnp.log1p(jnp.exp(-jnp.abs(pre)))
        bcum = jnp.dot(tri_ref[...], lf, preferred_element_type=F32, precision=lax.Precision.HIGHEST)
        pre_t = pre.T
        bcum_t = bcum.T
        qc_all = jnp.dot(qb, c_state[...].astype(BF16), preferred_element_type=F32)
        qn = q * n_state[...]
        num = jnp.zeros((CHUNK, HD), F32)
        kw = jnp.zeros((CHUNK, HD), F32)
        dec_row = jnp.zeros((1, HD), F32)
        for h in range(MLSTM_HEADS):
            head = (lane >= h * MLSTM_DIM) & (lane < (h + 1) * MLSTM_DIM)
            head1 = (lane1 >= h * MLSTM_DIM) & (lane1 < (h + 1) * MLSTM_DIM)
            fl = MLSTM_HEADS + h
            b_col = bcum[:, fl:fl + 1]
            b_row = bcum_t[fl:fl + 1, :]
            i_col = pre[:, h:h + 1]
            i_row = pre_t[h:h + 1, :]
            m_prev = m_state[h][0:1, 0:1]
            dm = jnp.where(tri, b_col - b_row + i_row, -jnp.inf)
            inter = b_col + m_prev
            mt = jnp.maximum(inter, jnp.max(dm, axis=-1, keepdims=True))
            w_intra = jnp.exp(dm - mt)
            w_inter = jnp.exp(inter - mt)
            qh = jnp.where(head, qb, jnp.zeros_like(qb))
            sm = lax.dot_general(qh, kb, NT_DIMS, preferred_element_type=F32) * w_intra
            num_h = jnp.dot(sm.astype(BF16), vb, preferred_element_type=F32) + w_inter * qc_all
            den = (jnp.sum(sm, axis=-1, keepdims=True)
                   + w_inter * jnp.sum(jnp.where(head, qn, 0.0), axis=-1, keepdims=True))
            num = jnp.where(head, num_h / jnp.maximum(jnp.abs(den), jnp.exp(-mt)), num)
            m_new = mt[CHUNK - 1:CHUNK, :]
            b_last = b_col[CHUNK - 1:CHUNK, :]
            wk = jnp.exp(b_last - b_col + i_col - m_new)
            dec = jnp.exp(b_last + m_prev - m_new)
            kw = jnp.where(head, k * wk, kw)
            dec_row = jnp.where(head1, dec, dec_row)
            m_state[h] = jnp.broadcast_to(m_new, (SUBLANES, LANES))
        upd = lax.dot_general(kw.astype(BF16), vb, TN_DIMS, preferred_element_type=F32)
        c_state[...] = c_state[...] * dec_row + jnp.where(blk_r == blk_c, upd, 0.0)
        n_state[...] = n_state[...] * dec_row + jnp.sum(kw, axis=0, keepdims=True)
        mu = jnp.zeros((CHUNK, HD), F32)
        for h in range(MLSTM_HEADS):
            head = (lane >= h * MLSTM_DIM) & (lane < (h + 1) * MLSTM_DIM)
            mu = jnp.where(head, jnp.sum(jnp.where(head, num, 0.0), axis=-1, keepdims=True) / MLSTM_DIM, mu)
        d = num - mu
        var = jnp.zeros((CHUNK, HD), F32)
        for h in range(MLSTM_HEADS):
            head = (lane >= h * MLSTM_DIM) & (lane < (h + 1) * MLSTM_DIM)
            var = jnp.where(head, jnp.sum(jnp.where(head, d * d, 0.0), axis=-1, keepdims=True) / MLSTM_DIM, var)
        y = d * lax.rsqrt(var + LN_EPS) * gain_ref[...]
        out_ref[rows, :] = y * jax.nn.sigmoid(o_ref[rows, :])


def _mlstm(z, conv_w, bias_row, gain, B, S):
    T = B * S
    rb = 512
    nsb = S // rb
    HD = MLSTM_HEADS * MLSTM_DIM
    zcol = lambda col, w: pl.BlockSpec((rb, w), lambda b, s: (b * nsb + s, col // w))
    const2 = lambda b, s: (0, 0)
    tri = jnp.tril(jnp.ones((CHUNK, CHUNK), F32))
    return pl.pallas_call(
        _mlstm_kernel,
        grid=(B, nsb),
        in_specs=[zcol(COL_MQ, HD), zcol(COL_MK, HD), zcol(COL_MV, HD), zcol(COL_MO, HD), zcol(COL_MG, LANES),
                  pl.BlockSpec((CONV_WIDTH, 2 * HD), const2), pl.BlockSpec((1, LANES), const2),
                  pl.BlockSpec((1, HD), const2), pl.BlockSpec((CHUNK, CHUNK), const2)],
        out_specs=pl.BlockSpec((rb, HD), lambda b, s: (b * nsb + s, 0)),
        out_shape=jax.ShapeDtypeStruct((T, HD), F32),
        scratch_shapes=[pltpu.VMEM((HD, HD), F32), pltpu.VMEM((1, HD), F32),
                        pltpu.VMEM((MLSTM_HEADS, SUBLANES, LANES), F32), pltpu.VMEM((SUBLANES, 2 * HD), F32)],
        compiler_params=_params("parallel", "arbitrary"),
        name="mlstm",
    )(z, z, z, z, z, conv_w, bias_row, gain, tri)


def _outproj_kernel(ret_ref, att_ref, mls_ref, x_ref, w_ref, g_ref, b_ref, o_ref):
    acc = jnp.dot(ret_ref[...].astype(BF16), w_ref[0:512, :], preferred_element_type=F32)
    acc = acc + jnp.dot(att_ref[...].astype(BF16), w_ref[512:768, :], preferred_element_type=F32)
    acc = acc + jnp.dot(mls_ref[...].astype(BF16), w_ref[768:1024, :], preferred_element_type=F32)
    o_ref[...] = _layer_norm(DN_ALPHA * x_ref[...] + acc, g_ref[...], b_ref[...])


def _outproj(ret, att, mls, x2, w_out, g, b):
    T = x2.shape[0]
    tm = 256
    row = lambda i: (i, 0)
    const = lambda i: (0, 0)
    return pl.pallas_call(
        _outproj_kernel,
        grid=(T // tm,),
        in_specs=[pl.BlockSpec((tm, 512), row), pl.BlockSpec((tm, 256), row), pl.BlockSpec((tm, 256), row),
                  pl.BlockSpec((tm, D_MODEL), row), pl.BlockSpec((D_MODEL, D_MODEL), const),
                  pl.BlockSpec((1, D_MODEL), const), pl.BlockSpec((1, D_MODEL), const)],
        out_specs=pl.BlockSpec((tm, D_MODEL), row),
        out_shape=jax.ShapeDtypeStruct((T, D_MODEL), F32),
        compiler_params=_params("parallel"),
        name="outproj_ln",
    )(ret, att, mls, x2, w_out, g, b)


def _top16(v, payload=None):
    n = v.shape[0]
    rid = lax.broadcasted_iota(jnp.int32, v.shape, 0)
    vals, outs = [], []
    for _ in range(PEER_TOPK):
        m = jnp.max(v, axis=0, keepdims=True)
        am = jnp.min(jnp.where(v == m, rid, n), axis=0, keepdims=True)
        sel = rid == am
        vals.append(m)
        if payload is None:
            outs.append(am)
        else:
            outs.append(jnp.sum(jnp.where(sel, payload, 0), axis=0, keepdims=True))
        v = jnp.where(sel, -jnp.inf, v)
    return jnp.concatenate(vals, axis=0), jnp.concatenate(outs, axis=0)


def _route_kernel(x_ref, wqt_ref, keys_ref, idx_ref, gate_ref):
    tt = x_ref.shape[0]
    xb = x_ref[...].astype(BF16)
    half = PEER_QDIM // 2
    brow = lax.broadcasted_iota(jnp.int32, (SUBLANES, tt), 0)
    idx_rows, gate_rows = [], []
    for h in range(PEER_HEADS):
        sub = []
        for p in range(2):
            hp = 2 * h + p
            qt = lax.dot_general(wqt_ref[hp * half:(hp + 1) * half, :], xb, NT_DIMS, preferred_element_type=F32)
            sc = jnp.dot(keys_ref[hp], qt.astype(BF16), preferred_element_type=F32)
            sub.append(_top16(sc))
        (s1, i1), (s2, i2) = sub
        cand = [s1[0:1] + s2]
        cidx = [i1[0:1] * PEER_NKEYS + i2]
        for a in range(1, PEER_TOPK):
            nb = PEER_TOPK // (a + 1)
            cand.append(jnp.where(brow < nb, s1[a:a + 1] + s2[0:SUBLANES], -jnp.inf))
            cidx.append(i1[a:a + 1] * PEER_NKEYS + i2[0:SUBLANES])
        top_s, top_e = _top16(jnp.concatenate(cand, axis=0), jnp.concatenate(cidx, axis=0))
        e = jnp.exp(top_s - top_s[0:1])
        gate_rows.append(e / jnp.sum(e, axis=0, keepdims=True))
        idx_rows.append(top_e)
    idx_ref[...] = jnp.concatenate(idx_rows, axis=0).T
    gate_ref[...] = jnp.concatenate(gate_rows, axis=0).T


def _route(x1, wqt, keys):
    T = x1.shape[0]
    tt = 256
    kk = PEER_HEADS * PEER_TOPK
    return pl.pallas_call(
        _route_kernel,
        grid=(T // tt,),
        in_specs=[pl.BlockSpec((tt, D_MODEL), lambda i: (i, 0)),
                  pl.BlockSpec(wqt.shape, lambda i: (0, 0)),
                  pl.BlockSpec(keys.shape, lambda i: (0, 0, 0))],
        out_specs=[pl.BlockSpec((tt, kk), lambda i: (i, 0)), pl.BlockSpec((tt, kk), lambda i: (i, 0))],
        out_shape=[jax.ShapeDtypeStruct((T, kk), jnp.int32), jax.ShapeDtypeStruct((T, kk), F32)],
        compiler_params=_params("parallel"),
        name="peer_route",
    )(x1, wqt, keys)


def _expert_table(u, v):
    ub = lax.bitcast_convert_type(u.astype(BF16), jnp.uint16).astype(jnp.uint32)
    vb = lax.bitcast_convert_type(v.astype(BF16), jnp.uint16).astype(jnp.uint32)
    return ((vb << 16) | ub).reshape(u.shape[0] * EXPERT_ROWS, LANES)


def _expert_kernel(idx_ref, x_ref, gate_ref, tab_ref, y_ref, buf, sem):
    kk = PEER_HEADS * PEER_TOPK
    n_rows = PEER_GROUP * kk
    n_groups = x_ref.shape[0] // PEER_GROUP

    def gather_copy(src_row, slot, dst_row):
        return pltpu.make_async_copy(
            tab_ref.at[pl.ds(pl.multiple_of(src_row * EXPERT_ROWS, EXPERT_ROWS), EXPERT_ROWS), :],
            buf.at[slot, pl.ds(pl.multiple_of(dst_row * EXPERT_ROWS, EXPERT_ROWS), EXPERT_ROWS), :],
            sem.at[slot])

    def issue(g, slot):
        def body(j, carry):
            t = j // kk
            gather_copy(idx_ref[g * PEER_GROUP + t, j - t * kk], slot, j).start()
            return carry
        lax.fori_loop(0, n_rows, body, 0, unroll=8)

    def wait(slot):
        pltpu.make_async_copy(tab_ref.at[pl.ds(0, n_rows * EXPERT_ROWS), :], buf.at[slot], sem.at[slot]).wait()

    row = lax.broadcasted_iota(jnp.int32, (PEER_GROUP, LANES), 0)

    def words(slot, t, s):
        return buf[slot, pl.ds(t * kk * EXPERT_ROWS + s, kk, stride=EXPERT_ROWS), :]

    def compute(g, slot):
        base = pl.multiple_of(g * PEER_GROUP, PEER_GROUP)
        xb = x_ref[pl.ds(base, PEER_GROUP), :].astype(BF16)
        a = jnp.zeros((PEER_GROUP, kk), F32)
        for t in range(PEER_GROUP):
            acc = jnp.zeros((PEER_GROUP, kk), F32)
            for s in range(EXPERT_ROWS):
                wu = pltpu.bitcast(words(slot, t, s) << 16, F32).astype(BF16)
                acc = acc + lax.dot_general(xb[:, s * LANES:(s + 1) * LANES], wu, NT_DIMS,
                                            preferred_element_type=F32)
            a = jnp.where(row == t, acc, a)
        act = 0.5 * a * (1.0 + lax.erf(a * (1.0 / math.sqrt(2.0))))
        cb = (act * gate_ref[pl.ds(base, PEER_GROUP), :]).astype(BF16)
        for s in range(EXPERT_ROWS):
            y = jnp.zeros((PEER_GROUP, LANES), F32)
            for t in range(PEER_GROUP):
                wv = pltpu.bitcast(words(slot, t, s) & jnp.uint32(0xFFFF0000), F32).astype(BF16)
                y = jnp.where(row == t, jnp.dot(cb, wv, preferred_element_type=F32), y)
            y_ref[pl.ds(base, PEER_GROUP), s * LANES:(s + 1) * LANES] = y

    issue(0, 0)

    def pair(i, carry):
        g0 = 2 * i
        issue(g0 + 1, 1)
        wait(0)
        compute(g0, 0)

        @pl.when(g0 + 2 < n_groups)
        def _():
            issue(g0 + 2, 0)

        wait(1)
        compute(g0 + 1, 1)
        return carry

    lax.fori_loop(0, n_groups // 2, pair, 0)


def _experts(idx, x1, gates, table):
    T = x1.shape[0]
    kk = PEER_HEADS * PEER_TOPK
    tb = PEER_TOKENS
    return pl.pallas_call(
        _expert_kernel,
        grid=(T // tb,),
        in_specs=[pl.BlockSpec((tb, kk), lambda i: (i, 0), memory_space=pltpu.SMEM),
                  pl.BlockSpec((tb, D_MODEL), lambda i: (i, 0)),
                  pl.BlockSpec((tb, kk), lambda i: (i, 0)),
                  pl.BlockSpec(memory_space=pl.ANY)],
        out_specs=pl.BlockSpec((tb, D_MODEL), lambda i: (i, 0)),
        out_shape=jax.ShapeDtypeStruct((T, D_MODEL), F32),
        scratch_shapes=[pltpu.VMEM((2, PEER_GROUP * kk * EXPERT_ROWS, LANES), jnp.uint32),
                        pltpu.SemaphoreType.DMA((2,))],
        compiler_params=_params("arbitrary"),
        name="peer_experts",
    )(idx, x1, gates, table)


def _final_kernel(x1_ref, y_ref, p_ref, wg_ref, bg_ref, wp_ref, g_ref, b_ref, o_ref):
    x1 = x1_ref[...]
    gate = jax.nn.sigmoid(jnp.dot(x1.astype(BF16), wg_ref[...], preferred_element_type=F32) + bg_ref[...])
    emb = jnp.dot(p_ref[...].astype(BF16), wp_ref[...], preferred_element_type=F32)
    o_ref[...] = _layer_norm(DN_ALPHA * x1 + y_ref[...] + gate * emb, g_ref[...], b_ref[...])


def _final(x1, y, p2, wg, bg, wp, g, b):
    T = x1.shape[0]
    tm = 256
    row = lambda i: (i, 0)
    const = lambda i: (0, 0)
    return pl.pallas_call(
        _final_kernel,
        grid=(T // tm,),
        in_specs=[pl.BlockSpec((tm, D_MODEL), row), pl.BlockSpec((tm, D_MODEL), row), pl.BlockSpec((tm, PLE_DIM), row),
                  pl.BlockSpec((D_MODEL, D_MODEL), const), pl.BlockSpec((1, D_MODEL), const),
                  pl.BlockSpec((PLE_DIM, D_MODEL), const), pl.BlockSpec((1, D_MODEL), const),
                  pl.BlockSpec((1, D_MODEL), const)],
        out_specs=pl.BlockSpec((tm, D_MODEL), row),
        out_shape=jax.ShapeDtypeStruct((T, D_MODEL), F32),
        compiler_params=_params("parallel"),
        name="ple_ln",
    )(x1, y, p2, wg, bg, wp, g, b)


def _layer(x2, p2, B, S, rot_tabs, ret_tabs, att_w, w_in, conv_w, b_if, ret_norm_g, mlstm_norm_g, w_out,
           ln1_g, ln1_b, peer_wq, peer_keys, peer_u, peer_v, ple_wg, ple_bg, ple_wp, ln2_g, ln2_b):
    row = lambda a: a.reshape(1, -1).astype(F32)
    w_pad = jnp.pad(w_in, ((0, 0), (0, Z_WIDTH - w_in.shape[1]))).astype(BF16)
    z = _inproj(x2, w_pad)
    rqk, aqk, av = _prep(z, rot_tabs, B, S)
    ret = _retention(rqk, z, row(ret_norm_g), ret_tabs, B, S)
    att = _attention(aqk, av, att_w, B, S)
    bias_row = jnp.pad(b_if.astype(F32), (0, LANES - b_if.shape[0])).reshape(1, LANES)
    mls = _mlstm(z, conv_w.astype(F32), bias_row, row(mlstm_norm_g), B, S)
    x1 = _outproj(ret, att, mls, x2, w_out.astype(BF16), row(ln1_g), row(ln1_b))
    keys = peer_keys.reshape(PEER_HEADS * 2, PEER_NKEYS, PEER_QDIM // 2).astype(BF16)
    idx, gates = _route(x1, peer_wq.T.astype(BF16), keys)
    y = _experts(idx, x1, gates, _expert_table(peer_u, peer_v))
    return _final(x1, y, p2, ple_wg.astype(BF16), row(ple_bg), ple_wp.astype(BF16), row(ln2_g), row(ln2_b))


def kernel(x, p, w_in, conv_w, b_if, ret_norm_g, mlstm_norm_g, w_out, ln1_g, ln1_b, peer_wq, peer_keys, peer_u,
           peer_v, ple_wg, ple_bg, ple_wp, ln2_g, ln2_b):
    B, S, D = x.shape
    rot_tabs = _rotary_tables(S)
    ret_tabs = _retention_tables()
    att_w = _attention_weights()
    x2 = x.reshape(B * S, D)
    for i in range(DEPTH):
        x2 = _layer(x2, p[i].reshape(B * S, PLE_DIM), B, S, rot_tabs, ret_tabs, att_w, w_in[i], conv_w[i], b_if[i],
                    ret_norm_g[i], mlstm_norm_g[i], w_out[i], ln1_g[i], ln1_b[i], peer_wq[i], peer_keys[i],
                    peer_u[i], peer_v[i], ple_wg[i], ple_bg[i], ple_wp[i], ln2_g[i], ln2_b[i])
    return x2.reshape(B, S, D)
```

```python
import functools
import math

import jax
import jax.numpy as jnp
from jax import lax
from jax.experimental import pallas as pl
from jax.experimental.pallas import tpu as pltpu

F32 = jnp.float32
BF16 = jnp.bfloat16

D_MODEL = 1024
DEPTH = 2
RET_HEADS, RET_DK, RET_DV, RET_THETA = 4, 64, 128, 10000.0
ATT_HEADS, ATT_DIM, ROPE_THETA = 4, 64, 500000.0
ROPE_DIMS = ATT_DIM // 4
DILATED_BRANCHES = ((128, 1), (512, 4), (2048, 16))
MLSTM_HEADS, MLSTM_DIM, CONV_WIDTH = 4, 64, 4
CHUNK = 128
PEER_HEADS, PEER_NKEYS, PEER_QDIM, PEER_TOPK = 8, 128, 256, 16
PEER_EXPERTS = PEER_NKEYS * PEER_NKEYS
PLE_DIM = 256
DN_ALPHA = (2 * DEPTH) ** 0.25
LN_EPS = 1e-5

Z_WIDTH = 3584
COL_RQ, COL_RK, COL_RV, COL_RG = 0, 256, 512, 1024
COL_AQ, COL_AK, COL_AV = 1536, 1792, 2048
COL_MQ, COL_MK, COL_MV, COL_MO, COL_MG = 2304, 2560, 2816, 3072, 3328

LANES = 128
SUBLANES = 8
VMEM_LIMIT = 56 * 1024 * 1024

ATT_WINDOW = 2048
ATT_KEYS = ATT_WINDOW + CHUNK

PEER_GROUP = 8
PEER_TOKENS = 64
EXPERT_ROWS = D_MODEL // LANES

NT_DIMS = (((1,), (1,)), ((), ()))
TN_DIMS = (((0,), (0,)), ((), ()))


def _params(*sem):
    return pltpu.CompilerParams(dimension_semantics=sem, vmem_limit_bytes=VMEM_LIMIT)


def _layer_norm(r, g, b):
    mu = jnp.mean(r, axis=-1, keepdims=True)
    d = r - mu
    var = jnp.mean(d * d, axis=-1, keepdims=True)
    return d * lax.rsqrt(var + LN_EPS) * g + b


def _inproj_kernel(x_ref, w_ref, z_ref):
    xb = x_ref[...].astype(BF16)
    for j in range(Z_WIDTH // 512):
        cols = slice(j * 512, (j + 1) * 512)
        z_ref[:, cols] = jnp.dot(xb, w_ref[:, cols], preferred_element_type=F32)


def _inproj(x2, w_pad):
    T = x2.shape[0]
    tm = 256
    return pl.pallas_call(
        _inproj_kernel,
        grid=(T // tm,),
        in_specs=[pl.BlockSpec((tm, D_MODEL), lambda i: (i, 0)),
                  pl.BlockSpec((D_MODEL, Z_WIDTH), lambda i: (0, 0))],
        out_specs=pl.BlockSpec((tm, Z_WIDTH), lambda i: (i, 0)),
        out_shape=jax.ShapeDtypeStruct((T, Z_WIDTH), F32),
        compiler_params=_params("parallel"),
        name="inproj",
    )(x2, w_pad)


def _prep_kernel(zr_ref, za_ref, zv_ref, cr_ref, sr_ref, ca_ref, sa_ref, rqk_ref, aqk_ref, av_ref):
    lane = lax.broadcasted_iota(jnp.int32, zr_ref.shape, 1) % 64
    width = zr_ref.shape[1]
    x = zr_ref[...]
    xs = jnp.where(lane < RET_DK // 2, pltpu.roll(x, width - RET_DK // 2, 1), pltpu.roll(x, RET_DK // 2, 1))
    rqk_ref[...] = (x * cr_ref[...] + xs * sr_ref[...]).astype(BF16)
    x = za_ref[...]
    xs = jnp.where(lane < ROPE_DIMS // 2, pltpu.roll(x, width - ROPE_DIMS // 2, 1), pltpu.roll(x, ROPE_DIMS // 2, 1))
    aqk_ref[...] = (x * ca_ref[...] + xs * sa_ref[...]).astype(BF16)
    av_ref[...] = zv_ref[...].astype(BF16)


def _prep(z, tabs, B, S):
    T = B * S
    rb = 512
    nsb = S // rb
    row = lambda s, b: (b * nsb + s, 0)
    tab = lambda s, b: (s, 0)
    return pl.pallas_call(
        _prep_kernel,
        grid=(nsb, B),
        in_specs=[pl.BlockSpec((rb, 512), lambda s, b: (b * nsb + s, COL_RQ // 512)),
                  pl.BlockSpec((rb, 512), lambda s, b: (b * nsb + s, COL_AQ // 512)),
                  pl.BlockSpec((rb, 256), lambda s, b: (b * nsb + s, COL_AV // 256)),
                  pl.BlockSpec((rb, 512), tab), pl.BlockSpec((rb, 512), tab),
                  pl.BlockSpec((rb, 512), tab), pl.BlockSpec((rb, 512), tab)],
        out_specs=[pl.BlockSpec((rb, 512), row), pl.BlockSpec((rb, 512), row), pl.BlockSpec((rb, 256), row)],
        out_shape=[jax.ShapeDtypeStruct((T, 512), BF16), jax.ShapeDtypeStruct((T, 512), BF16),
                   jax.ShapeDtypeStruct((T, 256), BF16)],
        compiler_params=_params("parallel", "parallel"),
        name="prep",
    )(z, z, z, *tabs)


def _rotary_tables(S):
    pos = jnp.arange(S, dtype=F32)

    def table(n_rot, theta, head_dim, heads):
        half = n_rot // 2
        inv = theta ** (-jnp.arange(half, dtype=F32) / half)
        ang = pos[:, None] * inv[None, :]
        cos, sin = jnp.cos(ang), jnp.sin(ang)
        rest = head_dim - n_rot
        c = jnp.concatenate([cos, cos, jnp.ones((S, rest), F32)], -1)
        s = jnp.concatenate([-sin, sin, jnp.zeros((S, rest), F32)], -1)
        return jnp.tile(c, (1, heads)), jnp.tile(s, (1, heads))

    cr, sr = table(RET_DK, RET_THETA, RET_DK, RET_HEADS)
    ca, sa = table(ROPE_DIMS, ROPE_THETA, ATT_DIM, ATT_HEADS)
    ks = RET_DK ** -0.5
    qs = ATT_DIM ** -0.5
    return (jnp.concatenate([cr, cr * ks], -1), jnp.concatenate([sr, sr * ks], -1),
            jnp.concatenate([ca * qs, ca], -1), jnp.concatenate([sa * qs, sa], -1))


def _ret_kernel(qk_ref, v_ref, g_ref, dec_ref, xi_ref, zeta_ref, gch_ref, gain_ref, o_ref, state):
    @pl.when(pl.program_id(1) == 0)
    def _():
        state[...] = jnp.zeros_like(state)

    nchunks = qk_ref.shape[0] // CHUNK
    hq = RET_HEADS * RET_DK
    for c in range(nchunks):
        rows = slice(c * CHUNK, (c + 1) * CHUNK)
        q = qk_ref[rows, 0:hq]
        k = qk_ref[rows, hq:2 * hq]
        qx = (q.astype(F32) * xi_ref[...]).astype(BF16)
        kz = (k.astype(F32) * zeta_ref[...]).astype(BF16)
        for h in range(RET_HEADS):
            dk = slice(h * RET_DK, (h + 1) * RET_DK)
            dv = slice(h * RET_DV, (h + 1) * RET_DV)
            vh = v_ref[rows, dv].astype(BF16)
            sc = lax.dot_general(q[:, dk], k[:, dk], NT_DIMS, preferred_element_type=F32) * dec_ref[h]
            r_prev = state[h]
            out = (jnp.dot(sc.astype(BF16), vh, preferred_element_type=F32)
                   + jnp.dot(qx[:, dk], r_prev.astype(BF16), preferred_element_type=F32))
            state[h] = gch_ref[h] * r_prev + lax.dot_general(kz[:, dk], vh, TN_DIMS, preferred_element_type=F32)
            mu = jnp.mean(out, axis=-1, keepdims=True)
            d = out - mu
            var = jnp.mean(d * d, axis=-1, keepdims=True)
            y = d * lax.rsqrt(var + LN_EPS) * gain_ref[:, dv]
            gate = g_ref[rows, dv]
            o_ref[rows, dv] = y * (gate * jax.nn.sigmoid(gate))


def _retention_tables():
    H, L = RET_HEADS, CHUNK
    lg = jnp.log(1.0 - 2.0 ** (-5.0 - jnp.arange(H, dtype=F32)))
    t = jnp.arange(L, dtype=F32)
    diff = t[:, None] - t[None, :]
    decay = jnp.where(diff >= 0, jnp.exp(lg[:, None, None] * jnp.maximum(diff, 0.0)), 0.0)
    zeta = jnp.exp(lg[:, None] * (L - 1 - t)[None, :])
    xi = jnp.exp(lg[:, None] * (t + 1)[None, :])
    gch = jnp.exp(lg * L)
    spread = lambda a: jnp.repeat(a.T, RET_DK, axis=1)
    return decay, spread(xi), spread(zeta), jnp.broadcast_to(gch[:, None, None], (H, RET_DK, RET_DV))


def _retention(rqk, z, gain, tabs, B, S):
    T = B * S
    rb = 512
    nsb = S // rb
    row = lambda b, s: (b * nsb + s, 0)
    const2 = lambda b, s: (0, 0)
    const3 = lambda b, s: (0, 0, 0)
    decay, xi, zeta, gch = tabs
    return pl.pallas_call(
        _ret_kernel,
        grid=(B, nsb),
        in_specs=[pl.BlockSpec((rb, 512), row),
                  pl.BlockSpec((rb, 512), lambda b, s: (b * nsb + s, COL_RV // 512)),
                  pl.BlockSpec((rb, 512), lambda b, s: (b * nsb + s, COL_RG // 512)),
                  pl.BlockSpec(decay.shape, const3), pl.BlockSpec(xi.shape, const2),
                  pl.BlockSpec(zeta.shape, const2), pl.BlockSpec(gch.shape, const3),
                  pl.BlockSpec((1, 512), const2)],
        out_specs=pl.BlockSpec((rb, 512), row),
        out_shape=jax.ShapeDtypeStruct((T, 512), F32),
        scratch_shapes=[pltpu.VMEM((RET_HEADS, RET_DK, RET_DV), F32)],
        compiler_params=_params("parallel", "arbitrary"),
        name="retention",
    )(rqk, z, z, decay, xi, zeta, gch, gain)


def _att_kernel(q_ref, k_ref, v_ref, w_ref, o_ref, kpad, vpad):
    S = q_ref.shape[0]
    kpad[0:ATT_WINDOW, :] = jnp.zeros((ATT_WINDOW, LANES), BF16)
    vpad[0:ATT_WINDOW, :] = jnp.zeros((ATT_WINDOW, LANES), BF16)
    kpad[ATT_WINDOW:ATT_WINDOW + S, :] = k_ref[...]
    vpad[ATT_WINDOW:ATT_WINDOW + S, :] = v_ref[...]
    lane = lax.broadcasted_iota(jnp.int32, (CHUNK, LANES), 1)
    col = lax.broadcasted_iota(jnp.int32, (CHUNK, ATT_KEYS), 1)

    def body(n, carry):
        base = pl.multiple_of(n * CHUNK, CHUNK)
        q = q_ref[pl.ds(base, CHUNK), :]
        kw = kpad[pl.ds(base, ATT_KEYS), :]
        vw = vpad[pl.ds(base, ATT_KEYS), :]
        wm = jnp.where(col >= ATT_WINDOW - base, w_ref[...], 0.0)
        out = jnp.zeros((CHUNK, LANES), F32)
        for h in range(2):
            head = (lane >= h * ATT_DIM) & (lane < (h + 1) * ATT_DIM)
            qh = jnp.where(head, q, jnp.zeros_like(q))
            s = lax.dot_general(qh, kw, NT_DIMS, preferred_element_type=F32)
            s = jnp.where(wm > 0.0, s, -1e30)
            m = jnp.max(s, axis=-1, keepdims=True)
            p = jnp.exp(s - m) * wm
            den = jnp.sum(p, axis=-1, keepdims=True)
            pv = jnp.dot(p.astype(BF16), vw, preferred_element_type=F32)
            out = jnp.where(head, pv / den, out)
        o_ref[pl.ds(base, CHUNK), :] = out
        return carry

    lax.fori_loop(0, S // CHUNK, body, 0)


def _attention_weights():
    r = jnp.arange(CHUNK)[:, None]
    c = jnp.arange(ATT_KEYS)[None, :]
    delta = r + ATT_WINDOW - c
    w = jnp.zeros((CHUNK, ATT_KEYS), F32)
    for window, dil in DILATED_BRANCHES:
        w = w + ((delta >= 0) & (delta <= window) & (delta % dil == 0)).astype(F32)
    return w


def _attention(aqk, av, wtab, B, S):
    T = B * S
    npair = ATT_HEADS // 2
    return pl.pallas_call(
        _att_kernel,
        grid=(B, npair),
        in_specs=[pl.BlockSpec((S, LANES), lambda b, h: (b, h)),
                  pl.BlockSpec((S, LANES), lambda b, h: (b, npair + h)),
                  pl.BlockSpec((S, LANES), lambda b, h: (b, h)),
                  pl.BlockSpec((CHUNK, ATT_KEYS), lambda b, h: (0, 0))],
        out_specs=pl.BlockSpec((S, LANES), lambda b, h: (b, h)),
        out_shape=jax.ShapeDtypeStruct((T, ATT_HEADS * ATT_DIM), F32),
        scratch_shapes=[pltpu.VMEM((ATT_WINDOW + S, LANES), BF16), pltpu.VMEM((ATT_WINDOW + S, LANES), BF16)],
        compiler_params=_params("parallel", "parallel"),
        name="dilated_attention",
    )(aqk, aqk, av, wtab)


def _mlstm_kernel(q_ref, k_ref, v_ref, o_ref, gate_ref, convw_ref, bias_ref, gain_ref, tri_ref, out_ref,
                  c_state, n_state, m_state, prev):
    HD = MLSTM_HEADS * MLSTM_DIM
    rb = q_ref.shape[0]

    @pl.when(pl.program_id(1) == 0)
    def _():
        c_state[...] = jnp.zeros_like(c_state)
        n_state[...] = jnp.zeros_like(n_state)
        m_state[...] = jnp.zeros_like(m_state)
        prev[...] = jnp.zeros_like(prev)

    cur = jnp.concatenate([q_ref[...], k_ref[...]], axis=1)
    full = jnp.concatenate([prev[...], cur], axis=0)
    prev[...] = cur[rb - SUBLANES:rb, :]
    acc = jnp.zeros_like(cur)
    for j in range(CONV_WIDTH):
        off = SUBLANES - (CONV_WIDTH - 1) + j
        acc = acc + convw_ref[j:j + 1, :] * full[off:off + rb, :]
    qk = acc * jax.nn.sigmoid(acc)
    q_all = qk[:, 0:HD]
    k_all = qk[:, HD:2 * HD] * (MLSTM_DIM ** -0.5)

    lane = lax.broadcasted_iota(jnp.int32, (CHUNK, HD), 1)
    lane1 = lax.broadcasted_iota(jnp.int32, (1, HD), 1)
    blk_r = lax.broadcasted_iota(jnp.int32, (HD, HD), 0) // MLSTM_DIM
    blk_c = lax.broadcasted_iota(jnp.int32, (HD, HD), 1) // MLSTM_DIM
    tri = tri_ref[...] > 0.0

    for c in range(rb // CHUNK):
        rows = slice(c * CHUNK, (c + 1) * CHUNK)
        q = q_all[rows]
        k = k_all[rows]
        v = v_ref[rows, :]
        qb, kb, vb = q.astype(BF16), k.astype(BF16), v.astype(BF16)
        pre = gate_ref[rows, :] + bias_ref[...]
        lf = jnp.minimum(pre, 0.0) - jnp.log1p(jnp.exp(-jnp.abs(pre)))
        bcum = jnp.dot(tri_ref[...], lf, preferred_element_type=F32, precision=lax.Precision.HIGHEST)
        pre_t = pre.T
        bcum_t = bcum.T
        qc_all = jnp.dot(qb, c_state[...].astype(BF16), preferred_element_type=F32)
        qn = q * n_state[...]
        num = jnp.zeros((CHUNK, HD), F32)
        kw = jnp.zeros((CHUNK, HD), F32)
        dec_row = jnp.zeros((1, HD), F32)
        for h in range(MLSTM_HEADS):
            head = (lane >= h * MLSTM_DIM) & (lane < (h + 1) * MLSTM_DIM)
            head1 = (lane1 >= h * MLSTM_DIM) & (lane1 < (h + 1) * MLSTM_DIM)
            fl = MLSTM_HEADS + h
            b_col = bcum[:, fl:fl + 1]
            b_row = bcum_t[fl:fl + 1, :]
            i_col = pre[:, h:h + 1]
            i_row = pre_t[h:h + 1, :]
            m_prev = m_state[h][0:1, 0:1]
            dm = jnp.where(tri, b_col - b_row + i_row, -jnp.inf)
            inter = b_col + m_prev
            mt = jnp.maximum(inter, jnp.max(dm, axis=-1, keepdims=True))
            w_intra = jnp.exp(dm - mt)
            w_inter = jnp.exp(inter - mt)
            qh = jnp.where(head, qb, jnp.zeros_like(qb))
            sm = lax.dot_general(qh, kb, NT_DIMS, preferred_element_type=F32) * w_intra
            num_h = jnp.dot(sm.astype(BF16), vb, preferred_element_type=F32) + w_inter * qc_all
            den = (jnp.sum(sm, axis=-1, keepdims=True)
                   + w_inter * jnp.sum(jnp.where(head, qn, 0.0), axis=-1, keepdims=True))
            num = jnp.where(head, num_h / jnp.maximum(jnp.abs(den), jnp.exp(-mt)), num)
            m_new = mt[CHUNK - 1:CHUNK, :]
            b_last = b_col[CHUNK - 1:CHUNK, :]
            wk = jnp.exp(b_last - b_col + i_col - m_new)
            dec = jnp.exp(b_last + m_prev - m_new)
            kw = jnp.where(head, k * wk, kw)
            dec_row = jnp.where(head1, dec, dec_row)
            m_state[h] = jnp.broadcast_to(m_new, (SUBLANES, LANES))
        upd = lax.dot_general(kw.astype(BF16), vb, TN_DIMS, preferred_element_type=F32)
        c_state[...] = c_state[...] * dec_row + jnp.where(blk_r == blk_c, upd, 0.0)
        n_state[...] = n_state[...] * dec_row + jnp.sum(kw, axis=0, keepdims=True)
        mu = jnp.zeros((CHUNK, HD), F32)
        for h in range(MLSTM_HEADS):
            head = (lane >= h * MLSTM_DIM) & (lane < (h + 1) * MLSTM_DIM)
            mu = jnp.where(head, jnp.sum(jnp.where(head, num, 0.0), axis=-1, keepdims=True) / MLSTM_DIM, mu)
        d = num - mu
        var = jnp.zeros((CHUNK, HD), F32)
        for h in range(MLSTM_HEADS):
            head = (lane >= h * MLSTM_DIM) & (lane < (h + 1) * MLSTM_DIM)
            var = jnp.where(head, jnp.sum(jnp.where(head, d * d, 0.0), axis=-1, keepdims=True) / MLSTM_DIM, var)
        y = d * lax.rsqrt(var + LN_EPS) * gain_ref[...]
        out_ref[rows, :] = y * jax.nn.sigmoid(o_ref[rows, :])


def _mlstm(z, conv_w, bias_row, gain, B, S):
    T = B * S
    rb = 512
    nsb = S // rb
    HD = MLSTM_HEADS * MLSTM_DIM
    zcol = lambda col, w: pl.BlockSpec((rb, w), lambda b, s: (b * nsb + s, col // w))
    const2 = lambda b, s: (0, 0)
    tri = jnp.tril(jnp.ones((CHUNK, CHUNK), F32))
    return pl.pallas_call(
        _mlstm_kernel,
        grid=(B, nsb),
        in_specs=[zcol(COL_MQ, HD), zcol(COL_MK, HD), zcol(COL_MV, HD), zcol(COL_MO, HD), zcol(COL_MG, LANES),
                  pl.BlockSpec((CONV_WIDTH, 2 * HD), const2), pl.BlockSpec((1, LANES), const2),
                  pl.BlockSpec((1, HD), const2), pl.BlockSpec((CHUNK, CHUNK), const2)],
        out_specs=pl.BlockSpec((rb, HD), lambda b, s: (b * nsb + s, 0)),
        out_shape=jax.ShapeDtypeStruct((T, HD), F32),
        scratch_shapes=[pltpu.VMEM((HD, HD), F32), pltpu.VMEM((1, HD), F32),
                        pltpu.VMEM((MLSTM_HEADS, SUBLANES, LANES), F32), pltpu.VMEM((SUBLANES, 2 * HD), F32)],
        compiler_params=_params("parallel", "arbitrary"),
        name="mlstm",
    )(z, z, z, z, z, conv_w, bias_row, gain, tri)


def _outproj_kernel(ret_ref, att_ref, mls_ref, x_ref, w_ref, g_ref, b_ref, o_ref):
    acc = jnp.dot(ret_ref[...].astype(BF16), w_ref[0:512, :], preferred_element_type=F32)
    acc = acc + jnp.dot(att_ref[...].astype(BF16), w_ref[512:768, :], preferred_element_type=F32)
    acc = acc + jnp.dot(mls_ref[...].astype(BF16), w_ref[768:1024, :], preferred_element_type=F32)
    o_ref[...] = _layer_norm(DN_ALPHA * x_ref[...] + acc, g_ref[...], b_ref[...])


def _outproj(ret, att, mls, x2, w_out, g, b):
    T = x2.shape[0]
    tm = 256
    row = lambda i: (i, 0)
    const = lambda i: (0, 0)
    return pl.pallas_call(
        _outproj_kernel,
        grid=(T // tm,),
        in_specs=[pl.BlockSpec((tm, 512), row), pl.BlockSpec((tm, 256), row), pl.BlockSpec((tm, 256), row),
                  pl.BlockSpec((tm, D_MODEL), row), pl.BlockSpec((D_MODEL, D_MODEL), const),
                  pl.BlockSpec((1, D_MODEL), const), pl.BlockSpec((1, D_MODEL), const)],
        out_specs=pl.BlockSpec((tm, D_MODEL), row),
        out_shape=jax.ShapeDtypeStruct((T, D_MODEL), F32),
        compiler_params=_params("parallel"),
        name="outproj_ln",
    )(ret, att, mls, x2, w_out, g, b)


def _top16(v, payload=None):
    n = v.shape[0]
    rid = lax.broadcasted_iota(jnp.int32, v.shape, 0)
    vals, outs = [], []
    for _ in range(PEER_TOPK):
        m = jnp.max(v, axis=0, keepdims=True)
        am = jnp.min(jnp.where(v == m, rid, n), axis=0, keepdims=True)
        sel = rid == am
        vals.append(m)
        if payload is None:
            outs.append(am)
        else:
            outs.append(jnp.sum(jnp.where(sel, payload, 0), axis=0, keepdims=True))
        v = jnp.where(sel, -jnp.inf, v)
    return jnp.concatenate(vals, axis=0), jnp.concatenate(outs, axis=0)


def _route_kernel(x_ref, wqt_ref, keys_ref, idx_ref, gate_ref):
    tt = x_ref.shape[0]
    xb = x_ref[...].astype(BF16)
    half = PEER_QDIM // 2
    brow = lax.broadcasted_iota(jnp.int32, (SUBLANES, tt), 0)
    idx_rows, gate_rows = [], []
    for h in range(PEER_HEADS):
        sub = []
        for p in range(2):
            hp = 2 * h + p
            qt = lax.dot_general(wqt_ref[hp * half:(hp + 1) * half, :], xb, NT_DIMS, preferred_element_type=F32)
            sc = jnp.dot(keys_ref[hp], qt.astype(BF16), preferred_element_type=F32)
            sub.append(_top16(sc))
        (s1, i1), (s2, i2) = sub
        cand = [s1[0:1] + s2]
        cidx = [i1[0:1] * PEER_NKEYS + i2]
        for a in range(1, PEER_TOPK):
            nb = PEER_TOPK // (a + 1)
            cand.append(jnp.where(brow < nb, s1[a:a + 1] + s2[0:SUBLANES], -jnp.inf))
            cidx.append(i1[a:a + 1] * PEER_NKEYS + i2[0:SUBLANES])
        top_s, top_e = _top16(jnp.concatenate(cand, axis=0), jnp.concatenate(cidx, axis=0))
        e = jnp.exp(top_s - top_s[0:1])
        gate_rows.append(e / jnp.sum(e, axis=0, keepdims=True))
        idx_rows.append(top_e)
    idx_ref[...] = jnp.concatenate(idx_rows, axis=0).T
    gate_ref[...] = jnp.concatenate(gate_rows, axis=0).T


def _route(x1, wqt, keys):
    T = x1.shape[0]
    tt = 256
    kk = PEER_HEADS * PEER_TOPK
    return pl.pallas_call(
        _route_kernel,
        grid=(T // tt,),
        in_specs=[pl.BlockSpec((tt, D_MODEL), lambda i: (i, 0)),
                  pl.BlockSpec(wqt.shape, lambda i: (0, 0)),
                  pl.BlockSpec(keys.shape, lambda i: (0, 0, 0))],
        out_specs=[pl.BlockSpec((tt, kk), lambda i: (i, 0)), pl.BlockSpec((tt, kk), lambda i: (i, 0))],
        out_shape=[jax.ShapeDtypeStruct((T, kk), jnp.int32), jax.ShapeDtypeStruct((T, kk), F32)],
        compiler_params=_params("parallel"),
        name="peer_route",
    )(x1, wqt, keys)


def _expert_table(u, v):
    ub = lax.bitcast_convert_type(u.astype(BF16), jnp.uint16).astype(jnp.uint32)
    vb = lax.bitcast_convert_type(v.astype(BF16), jnp.uint16).astype(jnp.uint32)
    return ((vb << 16) | ub).reshape(u.shape[0] * EXPERT_ROWS, LANES)


def _expert_kernel(idx_ref, idx_next_ref, x_ref, gate_ref, tab_ref, y_ref, buf, sem):
    kk = PEER_HEADS * PEER_TOPK
    n_rows = PEER_GROUP * kk
    n_groups = x_ref.shape[0] // PEER_GROUP
    step = pl.program_id(0)
    unroll = 8

    def gather_copy(src_row, slot, dst_row):
        return pltpu.make_async_copy(
            tab_ref.at[pl.ds(pl.multiple_of(src_row * EXPERT_ROWS, EXPERT_ROWS), EXPERT_ROWS), :],
            buf.at[slot, pl.ds(pl.multiple_of(dst_row * EXPERT_ROWS, EXPERT_ROWS), EXPERT_ROWS), :],
            sem.at[slot])

    def issue(ids_ref, g, slot):
        def body(i, carry):
            for u in range(unroll):
                j = i * unroll + u
                gather_copy(ids_ref[g * n_rows + j], slot, j).start(priority=u % 2)
            return carry
        lax.fori_loop(0, n_rows // unroll, body, 0)

    def wait(slot):
        pltpu.make_async_copy(tab_ref.at[pl.ds(0, n_rows * EXPERT_ROWS), :], buf.at[slot], sem.at[slot]).wait()

    lane = lax.broadcasted_iota(jnp.int32, (kk, LANES), 1)
    row = lax.broadcasted_iota(jnp.int32, (PEER_GROUP, LANES), 0)

    def words(slot, t, s):
        return buf[slot, pl.ds(t * kk * EXPERT_ROWS + s, kk, stride=EXPERT_ROWS), :]

    def compute(g, slot):
        base = pl.multiple_of(g * PEER_GROUP, PEER_GROUP)
        x8 = x_ref[pl.ds(base, PEER_GROUP), :]
        a_t = jnp.zeros((kk, LANES), F32)
        for t in range(PEER_GROUP):
            acc = jnp.zeros((kk, LANES), F32)
            for s in range(EXPERT_ROWS):
                wu = pltpu.bitcast(words(slot, t, s) << 16, F32)
                acc = acc + wu * x8[t:t + 1, s * LANES:(s + 1) * LANES]
            a_t = jnp.where(lane == t, jnp.sum(acc, axis=1, keepdims=True), a_t)
        a = a_t.T[0:PEER_GROUP, :]
        act = 0.5 * a * (1.0 + lax.erf(a * (1.0 / math.sqrt(2.0))))
        c = act * gate_ref[pl.ds(base, PEER_GROUP), :]
        c_t = jnp.concatenate([c, jnp.zeros((kk - PEER_GROUP, kk), F32)], axis=0).T
        y = [jnp.zeros((PEER_GROUP, LANES), F32)] * EXPERT_ROWS
        for t in range(PEER_GROUP):
            c_col = c_t[:, t:t + 1]
            for s in range(EXPERT_ROWS):
                wv = pltpu.bitcast(words(slot, t, s) & jnp.uint32(0xFFFF0000), F32)
                y[s] = jnp.where(row == t, jnp.sum(wv * c_col, axis=0, keepdims=True), y[s])
        for s in range(EXPERT_ROWS):
            y_ref[pl.ds(base, PEER_GROUP), s * LANES:(s + 1) * LANES] = y[s]

    @pl.when(step == 0)
    def _():
        issue(idx_ref, 0, 0)

    def pair(i, carry):
        g0 = 2 * i
        issue(idx_ref, g0 + 1, 1)
        wait(0)
        compute(g0, 0)

        @pl.when(g0 + 2 < n_groups)
        def _():
            issue(idx_ref, g0 + 2, 0)

        @pl.when((g0 + 2 == n_groups) & (step + 1 < pl.num_programs(0)))
        def _():
            issue(idx_next_ref, 0, 0)

        wait(1)
        compute(g0 + 1, 1)
        return carry

    lax.fori_loop(0, n_groups // 2, pair, 0)


def _experts(idx, x1, gates, table):
    T = x1.shape[0]
    kk = PEER_HEADS * PEER_TOPK
    tb = PEER_TOKENS
    nsteps = T // tb
    flat = idx.reshape(T * kk)
    return pl.pallas_call(
        _expert_kernel,
        grid=(nsteps,),
        in_specs=[pl.BlockSpec((tb * kk,), lambda i: (i,), memory_space=pltpu.SMEM),
                  pl.BlockSpec((tb * kk,), lambda i: (jnp.minimum(i + 1, nsteps - 1),), memory_space=pltpu.SMEM),
                  pl.BlockSpec((tb, D_MODEL), lambda i: (i, 0)),
                  pl.BlockSpec((tb, kk), lambda i: (i, 0)),
                  pl.BlockSpec(memory_space=pl.ANY)],
        out_specs=pl.BlockSpec((tb, D_MODEL), lambda i: (i, 0)),
        out_shape=jax.ShapeDtypeStruct((T, D_MODEL), F32),
        scratch_shapes=[pltpu.VMEM((2, PEER_GROUP * kk * EXPERT_ROWS, LANES), jnp.uint32),
                        pltpu.SemaphoreType.DMA((2,))],
        compiler_params=_params("arbitrary"),
        name="peer_experts",
    )(flat, flat, x1, gates, table)


def _final_kernel(x1_ref, y_ref, p_ref, wg_ref, bg_ref, wp_ref, g_ref, b_ref, o_ref):
    x1 = x1_ref[...]
    gate = jax.nn.sigmoid(jnp.dot(x1.astype(BF16), wg_ref[...], preferred_element_type=F32) + bg_ref[...])
    emb = jnp.dot(p_ref[...].astype(BF16), wp_ref[...], preferred_element_type=F32)
    o_ref[...] = _layer_norm(DN_ALPHA * x1 + y_ref[...] + gate * emb, g_ref[...], b_ref[...])


def _final(x1, y, p2, wg, bg, wp, g, b):
    T = x1.shape[0]
    tm = 256
    row = lambda i: (i, 0)
    const = lambda i: (0, 0)
    return pl.pallas_call(
        _final_kernel,
        grid=(T // tm,),
        in_specs=[pl.BlockSpec((tm, D_MODEL), row), pl.BlockSpec((tm, D_MODEL), row), pl.BlockSpec((tm, PLE_DIM), row),
                  pl.BlockSpec((D_MODEL, D_MODEL), const), pl.BlockSpec((1, D_MODEL), const),
                  pl.BlockSpec((PLE_DIM, D_MODEL), const), pl.BlockSpec((1, D_MODEL), const),
                  pl.BlockSpec((1, D_MODEL), const)],
        out_specs=pl.BlockSpec((tm, D_MODEL), row),
        out_shape=jax.ShapeDtypeStruct((T, D_MODEL), F32),
        compiler_params=_params("parallel"),
        name="ple_ln",
    )(x1, y, p2, wg, bg, wp, g, b)


def _layer(x2, p2, B, S, rot_tabs, ret_tabs, att_w, w_in, conv_w, b_if, ret_norm_g, mlstm_norm_g, w_out,
           ln1_g, ln1_b, peer_wq, peer_keys, peer_u, peer_v, ple_wg, ple_bg, ple_wp, ln2_g, ln2_b):
    row = lambda a: a.reshape(1, -1).astype(F32)
    w_pad = jnp.pad(w_in, ((0, 0), (0, Z_WIDTH - w_in.shape[1]))).astype(BF16)
    z = _inproj(x2, w_pad)
    rqk, aqk, av = _prep(z, rot_tabs, B, S)
    ret = _retention(rqk, z, row(ret_norm_g), ret_tabs, B, S)
    att = _attention(aqk, av, att_w, B, S)
    bias_row = jnp.pad(b_if.astype(F32), (0, LANES - b_if.shape[0])).reshape(1, LANES)
    mls = _mlstm(z, conv_w.astype(F32), bias_row, row(mlstm_norm_g), B, S)
    x1 = _outproj(ret, att, mls, x2, w_out.astype(BF16), row(ln1_g), row(ln1_b))
    keys = peer_keys.reshape(PEER_HEADS * 2, PEER_NKEYS, PEER_QDIM // 2).astype(BF16)
    idx, gates = _route(x1, peer_wq.T.astype(BF16), keys)
    y = _experts(idx, x1, gates, _expert_table(peer_u, peer_v))
    return _final(x1, y, p2, ple_wg.astype(BF16), row(ple_bg), ple_wp.astype(BF16), row(ln2_g), row(ln2_b))


def kernel(x, p, w_in, conv_w, b_if, ret_norm_g, mlstm_norm_g, w_out, ln1_g, ln1_b, peer_wq, peer_keys, peer_u,
           peer_v, ple_wg, ple_bg, ple_wp, ln2_g, ln2_b):
    B, S, D = x.shape
    rot_tabs = _rotary_tables(S)
    ret_tabs = _retention_tables()
    att_w = _attention_weights()
    x2 = x.reshape(B * S, D)
    for i in range(DEPTH):
        x2 = _layer(x2, p[i].reshape(B * S, PLE_DIM), B, S, rot_tabs, ret_tabs, att_w, w_in[i], conv_w[i], b_if[i],
                    ret_norm_g[i], mlstm_norm_g[i], w_out[i], ln1_g[i], ln1_b[i], peer_wq[i], peer_keys[i],
                    peer_u[i], peer_v[i], ple_wg[i], ple_bg[i], ple_wp[i], ln2_g[i], ln2_b[i])
    return x2.reshape(B, S, D)
```

```python
import functools
import math

import jax
import jax.numpy as jnp
from jax import lax
from jax.experimental import pallas as pl
from jax.experimental.pallas import tpu as pltpu

F32 = jnp.float32
BF16 = jnp.bfloat16

D_MODEL = 1024
DEPTH = 2
RET_HEADS, RET_DK, RET_DV, RET_THETA = 4, 64, 128, 10000.0
ATT_HEADS, ATT_DIM, ROPE_THETA = 4, 64, 500000.0
ROPE_DIMS = ATT_DIM // 4
DILATED_BRANCHES = ((128, 1), (512, 4), (2048, 16))
MLSTM_HEADS, MLSTM_DIM, CONV_WIDTH = 4, 64, 4
CHUNK = 128
PEER_HEADS, PEER_NKEYS, PEER_QDIM, PEER_TOPK = 8, 128, 256, 16
PEER_EXPERTS = PEER_NKEYS * PEER_NKEYS
PLE_DIM = 256
DN_ALPHA = (2 * DEPTH) ** 0.25
LN_EPS = 1e-5

Z_WIDTH = 3584
COL_RQ, COL_RK, COL_RV, COL_RG = 0, 256, 512, 1024
COL_AQ, COL_AK, COL_AV = 1536, 1792, 2048
COL_MQ, COL_MK, COL_MV, COL_MO, COL_MG = 2304, 2560, 2816, 3072, 3328

LANES = 128
SUBLANES = 8
VMEM_LIMIT = 56 * 1024 * 1024

ATT_WINDOW = 2048
ATT_KEYS = ATT_WINDOW + CHUNK

PEER_GROUP = 8
PEER_TOKENS = 64
EXPERT_ROWS = D_MODEL // LANES

NT_DIMS = (((1,), (1,)), ((), ()))
TN_DIMS = (((0,), (0,)), ((), ()))


def _params(*sem):
    return pltpu.CompilerParams(dimension_semantics=sem, vmem_limit_bytes=VMEM_LIMIT)


def _layer_norm(r, g, b):
    mu = jnp.mean(r, axis=-1, keepdims=True)
    d = r - mu
    var = jnp.mean(d * d, axis=-1, keepdims=True)
    return d * lax.rsqrt(var + LN_EPS) * g + b


def _inproj_kernel(x_ref, w_ref, z_ref):
    xb = x_ref[...].astype(BF16)
    for j in range(Z_WIDTH // 512):
        cols = slice(j * 512, (j + 1) * 512)
        z_ref[:, cols] = jnp.dot(xb, w_ref[:, cols], preferred_element_type=F32)


def _inproj(x2, w_pad):
    T = x2.shape[0]
    tm = 256
    return pl.pallas_call(
        _inproj_kernel,
        grid=(T // tm,),
        in_specs=[pl.BlockSpec((tm, D_MODEL), lambda i: (i, 0)),
                  pl.BlockSpec((D_MODEL, Z_WIDTH), lambda i: (0, 0))],
        out_specs=pl.BlockSpec((tm, Z_WIDTH), lambda i: (i, 0)),
        out_shape=jax.ShapeDtypeStruct((T, Z_WIDTH), F32),
        compiler_params=_params("parallel"),
        name="inproj",
    )(x2, w_pad)


def _prep_kernel(zr_ref, za_ref, zv_ref, cr_ref, sr_ref, ca_ref, sa_ref, rqk_ref, aqk_ref, av_ref):
    lane = lax.broadcasted_iota(jnp.int32, zr_ref.shape, 1) % 64
    width = zr_ref.shape[1]
    x = zr_ref[...]
    xs = jnp.where(lane < RET_DK // 2, pltpu.roll(x, width - RET_DK // 2, 1), pltpu.roll(x, RET_DK // 2, 1))
    rqk_ref[...] = (x * cr_ref[...] + xs * sr_ref[...]).astype(BF16)
    x = za_ref[...]
    xs = jnp.where(lane < ROPE_DIMS // 2, pltpu.roll(x, width - ROPE_DIMS // 2, 1), pltpu.roll(x, ROPE_DIMS // 2, 1))
    aqk_ref[...] = (x * ca_ref[...] + xs * sa_ref[...]).astype(BF16)
    av_ref[...] = zv_ref[...].astype(BF16)


def _prep(z, tabs, B, S):
    T = B * S
    rb = 512
    nsb = S // rb
    row = lambda s, b: (b * nsb + s, 0)
    tab = lambda s, b: (s, 0)
    return pl.pallas_call(
        _prep_kernel,
        grid=(nsb, B),
        in_specs=[pl.BlockSpec((rb, 512), lambda s, b: (b * nsb + s, COL_RQ // 512)),
                  pl.BlockSpec((rb, 512), lambda s, b: (b * nsb + s, COL_AQ // 512)),
                  pl.BlockSpec((rb, 256), lambda s, b: (b * nsb + s, COL_AV // 256)),
                  pl.BlockSpec((rb, 512), tab), pl.BlockSpec((rb, 512), tab),
                  pl.BlockSpec((rb, 512), tab), pl.BlockSpec((rb, 512), tab)],
        out_specs=[pl.BlockSpec((rb, 512), row), pl.BlockSpec((rb, 512), row), pl.BlockSpec((rb, 256), row)],
        out_shape=[jax.ShapeDtypeStruct((T, 512), BF16), jax.ShapeDtypeStruct((T, 512), BF16),
                   jax.ShapeDtypeStruct((T, 256), BF16)],
        compiler_params=_params("parallel", "parallel"),
        name="prep",
    )(z, z, z, *tabs)


def _rotary_tables(S):
    pos = jnp.arange(S, dtype=F32)

    def table(n_rot, theta, head_dim, heads):
        half = n_rot // 2
        inv = theta ** (-jnp.arange(half, dtype=F32) / half)
        ang = pos[:, None] * inv[None, :]
        cos, sin = jnp.cos(ang), jnp.sin(ang)
        rest = head_dim - n_rot
        c = jnp.concatenate([cos, cos, jnp.ones((S, rest), F32)], -1)
        s = jnp.concatenate([-sin, sin, jnp.zeros((S, rest), F32)], -1)
        return jnp.tile(c, (1, heads)), jnp.tile(s, (1, heads))

    cr, sr = table(RET_DK, RET_THETA, RET_DK, RET_HEADS)
    ca, sa = table(ROPE_DIMS, ROPE_THETA, ATT_DIM, ATT_HEADS)
    ks = RET_DK ** -0.5
    qs = ATT_DIM ** -0.5
    return (jnp.concatenate([cr, cr * ks], -1), jnp.concatenate([sr, sr * ks], -1),
            jnp.concatenate([ca * qs, ca], -1), jnp.concatenate([sa * qs, sa], -1))


def _ret_kernel(qk_ref, v_ref, g_ref, dec_ref, xi_ref, zeta_ref, gch_ref, gain_ref, o_ref, state):
    @pl.when(pl.program_id(1) == 0)
    def _():
        state[...] = jnp.zeros_like(state)

    nchunks = qk_ref.shape[0] // CHUNK
    hq = RET_HEADS * RET_DK
    for c in range(nchunks):
        rows = slice(c * CHUNK, (c + 1) * CHUNK)
        q = qk_ref[rows, 0:hq]
        k = qk_ref[rows, hq:2 * hq]
        qx = (q.astype(F32) * xi_ref[...]).astype(BF16)
        kz = (k.astype(F32) * zeta_ref[...]).astype(BF16)
        for h in range(RET_HEADS):
            dk = slice(h * RET_DK, (h + 1) * RET_DK)
            dv = slice(h * RET_DV, (h + 1) * RET_DV)
            vh = v_ref[rows, dv].astype(BF16)
            sc = lax.dot_general(q[:, dk], k[:, dk], NT_DIMS, preferred_element_type=F32) * dec_ref[h]
            r_prev = state[h]
            out = (jnp.dot(sc.astype(BF16), vh, preferred_element_type=F32)
                   + jnp.dot(qx[:, dk], r_prev.astype(BF16), preferred_element_type=F32))
            state[h] = gch_ref[h] * r_prev + lax.dot_general(kz[:, dk], vh, TN_DIMS, preferred_element_type=F32)
            mu = jnp.mean(out, axis=-1, keepdims=True)
            d = out - mu
            var = jnp.mean(d * d, axis=-1, keepdims=True)
            y = d * lax.rsqrt(var + LN_EPS) * gain_ref[:, dv]
            gate = g_ref[rows, dv]
            o_ref[rows, dv] = y * (gate * jax.nn.sigmoid(gate))


def _retention_tables():
    H, L = RET_HEADS, CHUNK
    lg = jnp.log(1.0 - 2.0 ** (-5.0 - jnp.arange(H, dtype=F32)))
    t = jnp.arange(L, dtype=F32)
    diff = t[:, None] - t[None, :]
    decay = jnp.where(diff >= 0, jnp.exp(lg[:, None, None] * jnp.maximum(diff, 0.0)), 0.0)
    zeta = jnp.exp(lg[:, None] * (L - 1 - t)[None, :])
    xi = jnp.exp(lg[:, None] * (t + 1)[None, :])
    gch = jnp.exp(lg * L)
    spread = lambda a: jnp.repeat(a.T, RET_DK, axis=1)
    return decay, spread(xi), spread(zeta), jnp.broadcast_to(gch[:, None, None], (H, RET_DK, RET_DV))


def _retention(rqk, z, gain, tabs, B, S):
    T = B * S
    rb = 512
    nsb = S // rb
    row = lambda b, s: (b * nsb + s, 0)
    const2 = lambda b, s: (0, 0)
    const3 = lambda b, s: (0, 0, 0)
    decay, xi, zeta, gch = tabs
    return pl.pallas_call(
        _ret_kernel,
        grid=(B, nsb),
        in_specs=[pl.BlockSpec((rb, 512), row),
                  pl.BlockSpec((rb, 512), lambda b, s: (b * nsb + s, COL_RV // 512)),
                  pl.BlockSpec((rb, 512), lambda b, s: (b * nsb + s, COL_RG // 512)),
                  pl.BlockSpec(decay.shape, const3), pl.BlockSpec(xi.shape, const2),
                  pl.BlockSpec(zeta.shape, const2), pl.BlockSpec(gch.shape, const3),
                  pl.BlockSpec((1, 512), const2)],
        out_specs=pl.BlockSpec((rb, 512), row),
        out_shape=jax.ShapeDtypeStruct((T, 512), F32),
        scratch_shapes=[pltpu.VMEM((RET_HEADS, RET_DK, RET_DV), F32)],
        compiler_params=_params("parallel", "arbitrary"),
        name="retention",
    )(rqk, z, z, decay, xi, zeta, gch, gain)


def _att_kernel(q_ref, k_ref, v_ref, w_ref, o_ref, kpad, vpad):
    S = q_ref.shape[0]
    kpad[0:ATT_WINDOW, :] = jnp.zeros((ATT_WINDOW, LANES), BF16)
    vpad[0:ATT_WINDOW, :] = jnp.zeros((ATT_WINDOW, LANES), BF16)
    kpad[ATT_WINDOW:ATT_WINDOW + S, :] = k_ref[...]
    vpad[ATT_WINDOW:ATT_WINDOW + S, :] = v_ref[...]
    lane = lax.broadcasted_iota(jnp.int32, (CHUNK, LANES), 1)
    col = lax.broadcasted_iota(jnp.int32, (CHUNK, ATT_KEYS), 1)

    def body(n, carry):
        base = pl.multiple_of(n * CHUNK, CHUNK)
        q = q_ref[pl.ds(base, CHUNK), :]
        kw = kpad[pl.ds(base, ATT_KEYS), :]
        vw = vpad[pl.ds(base, ATT_KEYS), :]
        wm = jnp.where(col >= ATT_WINDOW - base, w_ref[...], 0.0)
        out = jnp.zeros((CHUNK, LANES), F32)
        for h in range(2):
            head = (lane >= h * ATT_DIM) & (lane < (h + 1) * ATT_DIM)
            qh = jnp.where(head, q, jnp.zeros_like(q))
            s = lax.dot_general(qh, kw, NT_DIMS, preferred_element_type=F32)
            s = jnp.where(wm > 0.0, s, -1e30)
            m = jnp.max(s, axis=-1, keepdims=True)
            p = jnp.exp(s - m) * wm
            den = jnp.sum(p, axis=-1, keepdims=True)
            pv = jnp.dot(p.astype(BF16), vw, preferred_element_type=F32)
            out = jnp.where(head, pv / den, out)
        o_ref[pl.ds(base, CHUNK), :] = out
        return carry

    lax.fori_loop(0, S // CHUNK, body, 0)


def _attention_weights():
    r = jnp.arange(CHUNK)[:, None]
    c = jnp.arange(ATT_KEYS)[None, :]
    delta = r + ATT_WINDOW - c
    w = jnp.zeros((CHUNK, ATT_KEYS), F32)
    for window, dil in DILATED_BRANCHES:
        w = w + ((delta >= 0) & (delta <= window) & (delta % dil == 0)).astype(F32)
    return w


def _attention(aqk, av, wtab, B, S):
    T = B * S
    npair = ATT_HEADS // 2
    return pl.pallas_call(
        _att_kernel,
        grid=(B, npair),
        in_specs=[pl.BlockSpec((S, LANES), lambda b, h: (b, h)),
                  pl.BlockSpec((S, LANES), lambda b, h: (b, npair + h)),
                  pl.BlockSpec((S, LANES), lambda b, h: (b, h)),
                  pl.BlockSpec((CHUNK, ATT_KEYS), lambda b, h: (0, 0))],
        out_specs=pl.BlockSpec((S, LANES), lambda b, h: (b, h)),
        out_shape=jax.ShapeDtypeStruct((T, ATT_HEADS * ATT_DIM), F32),
        scratch_shapes=[pltpu.VMEM((ATT_WINDOW + S, LANES), BF16), pltpu.VMEM((ATT_WINDOW + S, LANES), BF16)],
        compiler_params=_params("parallel", "parallel"),
        name="dilated_attention",
    )(aqk, aqk, av, wtab)


def _mlstm_kernel(q_ref, k_ref, v_ref, o_ref, gate_ref, convw_ref, bias_ref, gain_ref, tri_ref, out_ref,
                  c_state, n_state, m_state, prev):
    HD = MLSTM_HEADS * MLSTM_DIM
    rb = q_ref.shape[0]

    @pl.when(pl.program_id(1) == 0)
    def _():
        c_state[...] = jnp.zeros_like(c_state)
        n_state[...] = jnp.zeros_like(n_state)
        m_state[...] = jnp.zeros_like(m_state)
        prev[...] = jnp.zeros_like(prev)

    cur = jnp.concatenate([q_ref[...], k_ref[...]], axis=1)
    full = jnp.concatenate([prev[...], cur], axis=0)
    prev[...] = cur[rb - SUBLANES:rb, :]
    acc = jnp.zeros_like(cur)
    for j in range(CONV_WIDTH):
        off = SUBLANES - (CONV_WIDTH - 1) + j
        acc = acc + convw_ref[j:j + 1, :] * full[off:off + rb, :]
    qk = acc * jax.nn.sigmoid(acc)
    q_all = qk[:, 0:HD]
    k_all = qk[:, HD:2 * HD] * (MLSTM_DIM ** -0.5)

    lane = lax.broadcasted_iota(jnp.int32, (CHUNK, HD), 1)
    lane1 = lax.broadcasted_iota(jnp.int32, (1, HD), 1)
    blk_r = lax.broadcasted_iota(jnp.int32, (HD, HD), 0) // MLSTM_DIM
    blk_c = lax.broadcasted_iota(jnp.int32, (HD, HD), 1) // MLSTM_DIM
    tri = tri_ref[...] > 0.0

    for c in range(rb // CHUNK):
        rows = slice(c * CHUNK, (c + 1) * CHUNK)
        q = q_all[rows]
        k = k_all[rows]
        v = v_ref[rows, :]
        qb, kb, vb = q.astype(BF16), k.astype(BF16), v.astype(BF16)
        pre = gate_ref[rows, :] + bias_ref[...]
        lf = jnp.minimum(pre, 0.0) - jnp.log1p(jnp.exp(-jnp.abs(pre)))
        bcum = jnp.dot(tri_ref[...], lf, preferred_element_type=F32, precision=lax.Precision.HIGHEST)
        pre_t = pre.T
        bcum_t = bcum.T
        qc_all = jnp.dot(qb, c_state[...].astype(BF16), preferred_element_type=F32)
        qn = q * n_state[...]
        num = jnp.zeros((CHUNK, HD), F32)
        kw = jnp.zeros((CHUNK, HD), F32)
        dec_row = jnp.zeros((1, HD), F32)
        for h in range(MLSTM_HEADS):
            head = (lane >= h * MLSTM_DIM) & (lane < (h + 1) * MLSTM_DIM)
            head1 = (lane1 >= h * MLSTM_DIM) & (lane1 < (h + 1) * MLSTM_DIM)
            fl = MLSTM_HEADS + h
            b_col = bcum[:, fl:fl + 1]
            b_row = bcum_t[fl:fl + 1, :]
            i_col = pre[:, h:h + 1]
            i_row = pre_t[h:h + 1, :]
            m_prev = m_state[h][0:1, 0:1]
            dm = jnp.where(tri, b_col - b_row + i_row, -jnp.inf)
            inter = b_col + m_prev
            mt = jnp.maximum(inter, jnp.max(dm, axis=-1, keepdims=True))
            w_intra = jnp.exp(dm - mt)
            w_inter = jnp.exp(inter - mt)
            qh = jnp.where(head, qb, jnp.zeros_like(qb))
            sm = lax.dot_general(qh, kb, NT_DIMS, preferred_element_type=F32) * w_intra
            num_h = jnp.dot(sm.astype(BF16), vb, preferred_element_type=F32) + w_inter * qc_all
            den = (jnp.sum(sm, axis=-1, keepdims=True)
                   + w_inter * jnp.sum(jnp.where(head, qn, 0.0), axis=-1, keepdims=True))
            num = jnp.where(head, num_h / jnp.maximum(jnp.abs(den), jnp.exp(-mt)), num)
            m_new = mt[CHUNK - 1:CHUNK, :]
            b_last = b_col[CHUNK - 1:CHUNK, :]
            wk = jnp.exp(b_last - b_col + i_col - m_new)
            dec = jnp.exp(b_last + m_prev - m_new)
            kw = jnp.where(head, k * wk, kw)
            dec_row = jnp.where(head1, dec, dec_row)
            m_state[h] = jnp.broadcast_to(m_new, (SUBLANES, LANES))
        upd = lax.dot_general(kw.astype(BF16), vb, TN_DIMS, preferred_element_type=F32)
        c_state[...] = c_state[...] * dec_row + jnp.where(blk_r == blk_c, upd, 0.0)
        n_state[...] = n_state[...] * dec_row + jnp.sum(kw, axis=0, keepdims=True)
        mu = jnp.zeros((CHUNK, HD), F32)
        for h in range(MLSTM_HEADS):
            head = (lane >= h * MLSTM_DIM) & (lane < (h + 1) * MLSTM_DIM)
            mu = jnp.where(head, jnp.sum(jnp.where(head, num, 0.0), axis=-1, keepdims=True) / MLSTM_DIM, mu)
        d = num - mu
        var = jnp.zeros((CHUNK, HD), F32)
        for h in range(MLSTM_HEADS):
            head = (lane >= h * MLSTM_DIM) & (lane < (h + 1) * MLSTM_DIM)
            var = jnp.where(head, jnp.sum(jnp.where(head, d * d, 0.0), axis=-1, keepdims=True) / MLSTM_DIM, var)
        y = d * lax.rsqrt(var + LN_EPS) * gain_ref[...]
        out_ref[rows, :] = y * jax.nn.sigmoid(o_ref[rows, :])


def _mlstm(z, conv_w, bias_row, gain, B, S):
    T = B * S
    rb = 512
    nsb = S // rb
    HD = MLSTM_HEADS * MLSTM_DIM
    zcol = lambda col, w: pl.BlockSpec((rb, w), lambda b, s: (b * nsb + s, col // w))
    const2 = lambda b, s: (0, 0)
    tri = jnp.tril(jnp.ones((CHUNK, CHUNK), F32))
    return pl.pallas_call(
        _mlstm_kernel,
        grid=(B, nsb),
        in_specs=[zcol(COL_MQ, HD), zcol(COL_MK, HD), zcol(COL_MV, HD), zcol(COL_MO, HD), zcol(COL_MG, LANES),
                  pl.BlockSpec((CONV_WIDTH, 2 * HD), const2), pl.BlockSpec((1, LANES), const2),
                  pl.BlockSpec((1, HD), const2), pl.BlockSpec((CHUNK, CHUNK), const2)],
        out_specs=pl.BlockSpec((rb, HD), lambda b, s: (b * nsb + s, 0)),
        out_shape=jax.ShapeDtypeStruct((T, HD), F32),
        scratch_shapes=[pltpu.VMEM((HD, HD), F32), pltpu.VMEM((1, HD), F32),
                        pltpu.VMEM((MLSTM_HEADS, SUBLANES, LANES), F32), pltpu.VMEM((SUBLANES, 2 * HD), F32)],
        compiler_params=_params("parallel", "arbitrary"),
        name="mlstm",
    )(z, z, z, z, z, conv_w, bias_row, gain, tri)


def _outproj_kernel(ret_ref, att_ref, mls_ref, x_ref, w_ref, g_ref, b_ref, o_ref):
    acc = jnp.dot(ret_ref[...].astype(BF16), w_ref[0:512, :], preferred_element_type=F32)
    acc = acc + jnp.dot(att_ref[...].astype(BF16), w_ref[512:768, :], preferred_element_type=F32)
    acc = acc + jnp.dot(mls_ref[...].astype(BF16), w_ref[768:1024, :], preferred_element_type=F32)
    o_ref[...] = _layer_norm(DN_ALPHA * x_ref[...] + acc, g_ref[...], b_ref[...])


def _outproj(ret, att, mls, x2, w_out, g, b):
    T = x2.shape[0]
    tm = 256
    row = lambda i: (i, 0)
    const = lambda i: (0, 0)
    return pl.pallas_call(
        _outproj_kernel,
        grid=(T // tm,),
        in_specs=[pl.BlockSpec((tm, 512), row), pl.BlockSpec((tm, 256), row), pl.BlockSpec((tm, 256), row),
                  pl.BlockSpec((tm, D_MODEL), row), pl.BlockSpec((D_MODEL, D_MODEL), const),
                  pl.BlockSpec((1, D_MODEL), const), pl.BlockSpec((1, D_MODEL), const)],
        out_specs=pl.BlockSpec((tm, D_MODEL), row),
        out_shape=jax.ShapeDtypeStruct((T, D_MODEL), F32),
        compiler_params=_params("parallel"),
        name="outproj_ln",
    )(ret, att, mls, x2, w_out, g, b)


def _top16(vs, payloads=None):
    vs = list(vs)
    n = vs[0].shape[0]
    rid = lax.broadcasted_iota(jnp.int32, vs[0].shape, 0).astype(F32)
    vals = [[] for _ in vs]
    outs = [[] for _ in vs]
    for _ in range(PEER_TOPK):
        for c, v in enumerate(vs):
            m = jnp.max(v, axis=0, keepdims=True)
            am = jnp.min(jnp.where(v == m, rid, float(n)), axis=0, keepdims=True)
            sel = rid == am
            vals[c].append(m)
            if payloads is None:
                outs[c].append(am)
            else:
                outs[c].append(jnp.sum(jnp.where(sel, payloads[c], 0.0), axis=0, keepdims=True))
            vs[c] = jnp.where(sel, -jnp.inf, v)
    return [(jnp.concatenate(a, axis=0), jnp.concatenate(b, axis=0)) for a, b in zip(vals, outs)]


def _pair_tables():
    pairs = [(a, b) for a in range(PEER_TOPK) for b in range(PEER_TOPK // (a + 1))]
    rows = -(-len(pairs) // SUBLANES) * SUBLANES
    sel_a = [[1.0 if r < len(pairs) and pairs[r][0] == c else 0.0 for c in range(PEER_TOPK)] for r in range(rows)]
    sel_b = [[1.0 if r < len(pairs) and pairs[r][1] == c else 0.0 for c in range(PEER_TOPK)] for r in range(rows)]
    return jnp.asarray(sel_a, F32), jnp.asarray(sel_b, F32), len(pairs)


def _route_kernel(n_pairs, x_ref, wqt_ref, keys_ref, sela_ref, selb_ref, idx_ref, gate_ref):
    tt = x_ref.shape[0]
    xb = x_ref[...].astype(BF16)
    half = PEER_QDIM // 2
    pair_row = lax.broadcasted_iota(jnp.int32, (sela_ref.shape[0], tt), 0)
    pick = functools.partial(jnp.dot, preferred_element_type=F32, precision=lax.Precision.HIGHEST)
    idx_rows, gate_rows = [], []
    heads_per_pass = 2
    for h0 in range(0, PEER_HEADS, heads_per_pass):
        scores = []
        for hp in range(2 * h0, 2 * (h0 + heads_per_pass)):
            qt = lax.dot_general(wqt_ref[hp * half:(hp + 1) * half, :], xb, NT_DIMS, preferred_element_type=F32)
            scores.append(jnp.dot(keys_ref[hp], qt.astype(BF16), preferred_element_type=F32))
        sub = _top16(scores)
        cands, cidxs = [], []
        for h in range(heads_per_pass):
            (s1, i1), (s2, i2) = sub[2 * h], sub[2 * h + 1]
            cand = pick(sela_ref[...], s1) + pick(selb_ref[...], s2)
            cands.append(jnp.where(pair_row < n_pairs, cand, -jnp.inf))
            cidxs.append(pick(sela_ref[...], i1) * PEER_NKEYS + pick(selb_ref[...], i2))
        for top_s, top_e in _top16(cands, cidxs):
            e = jnp.exp(top_s - top_s[0:1])
            gate_rows.append(e / jnp.sum(e, axis=0, keepdims=True))
            idx_rows.append(top_e.astype(jnp.int32))
    idx_ref[...] = jnp.concatenate(idx_rows, axis=0).T
    gate_ref[...] = jnp.concatenate(gate_rows, axis=0).T


def _route(x1, wqt, keys):
    T = x1.shape[0]
    tt = 512
    kk = PEER_HEADS * PEER_TOPK
    sel_a, sel_b, n_pairs = _pair_tables()
    return pl.pallas_call(
        functools.partial(_route_kernel, n_pairs),
        grid=(T // tt,),
        in_specs=[pl.BlockSpec((tt, D_MODEL), lambda i: (i, 0)),
                  pl.BlockSpec(wqt.shape, lambda i: (0, 0)),
                  pl.BlockSpec(keys.shape, lambda i: (0, 0, 0)),
                  pl.BlockSpec(sel_a.shape, lambda i: (0, 0)),
                  pl.BlockSpec(sel_b.shape, lambda i: (0, 0))],
        out_specs=[pl.BlockSpec((tt, kk), lambda i: (i, 0)), pl.BlockSpec((tt, kk), lambda i: (i, 0))],
        out_shape=[jax.ShapeDtypeStruct((T, kk), jnp.int32), jax.ShapeDtypeStruct((T, kk), F32)],
        compiler_params=_params("parallel"),
        name="peer_route",
    )(x1, wqt, keys, sel_a, sel_b)


def _expert_table(u, v):
    ub = lax.bitcast_convert_type(u.astype(BF16), jnp.uint16).astype(jnp.uint32)
    vb = lax.bitcast_convert_type(v.astype(BF16), jnp.uint16).astype(jnp.uint32)
    return ((vb << 16) | ub).reshape(u.shape[0] * EXPERT_ROWS, LANES)


def _expert_kernel(ids_ref, x_ref, gate_ref, tab_ref, y_ref, buf, sem):
    kk = PEER_HEADS * PEER_TOPK
    n_rows = PEER_GROUP * kk
    n_groups = x_ref.shape[0] // PEER_GROUP
    step = pl.program_id(0)
    burst = n_rows // (2 * PEER_GROUP * EXPERT_ROWS)

    def gather_copy(src_row, slot, dst_row):
        return pltpu.make_async_copy(
            tab_ref.at[pl.ds(pl.multiple_of(src_row * EXPERT_ROWS, EXPERT_ROWS), EXPERT_ROWS), :],
            buf.at[slot, pl.ds(pl.multiple_of(dst_row * EXPERT_ROWS, EXPERT_ROWS), EXPERT_ROWS), :],
            sem.at[slot])

    def start_rows(g, slot, first, count):
        for u in range(count):
            gather_copy(ids_ref[g * n_rows + first + u], slot, first + u).start(priority=u % 2)

    def wait(slot):
        pltpu.make_async_copy(tab_ref.at[pl.ds(0, n_rows * EXPERT_ROWS), :], buf.at[slot], sem.at[slot]).wait()

    lane = lax.broadcasted_iota(jnp.int32, (kk, LANES), 1)
    row = lax.broadcasted_iota(jnp.int32, (PEER_GROUP, LANES), 0)

    def words(slot, t, s):
        return buf[slot, pl.ds(t * kk * EXPERT_ROWS + s, kk, stride=EXPERT_ROWS), :]

    @pl.when(step == 0)
    def _():
        def body(i, carry):
            for u in range(burst):
                j = i * burst + u
                gather_copy(ids_ref[j], 0, j).start(priority=u % 2)
            return carry
        lax.fori_loop(0, n_rows // burst, body, 0)

    def group(g, carry):
        slot = lax.rem(g, 2)
        nxt = 1 - slot
        wait(slot)
        base = pl.multiple_of(g * PEER_GROUP, PEER_GROUP)
        x8 = x_ref[pl.ds(base, PEER_GROUP), :]
        a_t = jnp.zeros((kk, LANES), F32)
        for t in range(PEER_GROUP):
            acc = jnp.zeros((kk, LANES), F32)
            for s in range(EXPERT_ROWS):
                start_rows(g + 1, nxt, (t * EXPERT_ROWS + s) * burst, burst)
                wu = pltpu.bitcast(words(slot, t, s) << 16, F32)
                acc = acc + wu * x8[t:t + 1, s * LANES:(s + 1) * LANES]
            a_t = jnp.where(lane == t, jnp.sum(acc, axis=1, keepdims=True), a_t)
        a = a_t.T[0:PEER_GROUP, :]
        act = 0.5 * a * (1.0 + lax.erf(a * (1.0 / math.sqrt(2.0))))
        c = act * gate_ref[pl.ds(base, PEER_GROUP), :]
        c_t = jnp.concatenate([c, jnp.zeros((kk - PEER_GROUP, kk), F32)], axis=0).T
        y = [jnp.zeros((PEER_GROUP, LANES), F32)] * EXPERT_ROWS
        for t in range(PEER_GROUP):
            c_col = c_t[:, t:t + 1]
            for s in range(EXPERT_ROWS):
                start_rows(g + 1, nxt, n_rows // 2 + (t * EXPERT_ROWS + s) * burst, burst)
                wv = pltpu.bitcast(words(slot, t, s) & jnp.uint32(0xFFFF0000), F32)
                y[s] = jnp.where(row == t, jnp.sum(wv * c_col, axis=0, keepdims=True), y[s])
        for s in range(EXPERT_ROWS):
            y_ref[pl.ds(base, PEER_GROUP), s * LANES:(s + 1) * LANES] = y[s]
        return carry

    lax.fori_loop(0, n_groups, group, 0)

    @pl.when(step == pl.num_programs(0) - 1)
    def _():
        wait(0)


def _experts(idx, x1, gates, table):
    T = x1.shape[0]
    kk = PEER_HEADS * PEER_TOPK
    tb = PEER_TOKENS
    nsteps = T // tb
    n_rows = PEER_GROUP * kk
    assert T % tb == 0 and (tb // PEER_GROUP) % 2 == 0
    ids = idx.reshape(nsteps, tb * kk)
    ids = jnp.concatenate([ids, jnp.roll(ids[:, :n_rows], -1, axis=0)], axis=1).reshape(nsteps * (tb * kk + n_rows))
    return pl.pallas_call(
        _expert_kernel,
        grid=(nsteps,),
        in_specs=[pl.BlockSpec((tb * kk + n_rows,), lambda i: (i,), memory_space=pltpu.SMEM),
                  pl.BlockSpec((tb, D_MODEL), lambda i: (i, 0)),
                  pl.BlockSpec((tb, kk), lambda i: (i, 0)),
                  pl.BlockSpec(memory_space=pl.ANY)],
        out_specs=pl.BlockSpec((tb, D_MODEL), lambda i: (i, 0)),
        out_shape=jax.ShapeDtypeStruct((T, D_MODEL), F32),
        scratch_shapes=[pltpu.VMEM((2, n_rows * EXPERT_ROWS, LANES), jnp.uint32),
                        pltpu.SemaphoreType.DMA((2,))],
        compiler_params=_params("arbitrary"),
        name="peer_experts",
    )(ids, x1, gates, table)


def _final_kernel(x1_ref, y_ref, p_ref, wg_ref, bg_ref, wp_ref, g_ref, b_ref, o_ref):
    x1 = x1_ref[...]
    gate = jax.nn.sigmoid(jnp.dot(x1.astype(BF16), wg_ref[...], preferred_element_type=F32) + bg_ref[...])
    emb = jnp.dot(p_ref[...].astype(BF16), wp_ref[...], preferred_element_type=F32)
    o_ref[...] = _layer_norm(DN_ALPHA * x1 + y_ref[...] + gate * emb, g_ref[...], b_ref[...])


def _final(x1, y, p2, wg, bg, wp, g, b):
    T = x1.shape[0]
    tm = 256
    row = lambda i: (i, 0)
    const = lambda i: (0, 0)
    return pl.pallas_call(
        _final_kernel,
        grid=(T // tm,),
        in_specs=[pl.BlockSpec((tm, D_MODEL), row), pl.BlockSpec((tm, D_MODEL), row), pl.BlockSpec((tm, PLE_DIM), row),
                  pl.BlockSpec((D_MODEL, D_MODEL), const), pl.BlockSpec((1, D_MODEL), const),
                  pl.BlockSpec((PLE_DIM, D_MODEL), const), pl.BlockSpec((1, D_MODEL), const),
                  pl.BlockSpec((1, D_MODEL), const)],
        out_specs=pl.BlockSpec((tm, D_MODEL), row),
        out_shape=jax.ShapeDtypeStruct((T, D_MODEL), F32),
        compiler_params=_params("parallel"),
        name="ple_ln",
    )(x1, y, p2, wg, bg, wp, g, b)


def _layer(x2, p2, B, S, rot_tabs, ret_tabs, att_w, w_in, conv_w, b_if, ret_norm_g, mlstm_norm_g, w_out,
           ln1_g, ln1_b, peer_wq, peer_keys, peer_u, peer_v, ple_wg, ple_bg, ple_wp, ln2_g, ln2_b):
    row = lambda a: a.reshape(1, -1).astype(F32)
    w_pad = jnp.pad(w_in, ((0, 0), (0, Z_WIDTH - w_in.shape[1]))).astype(BF16)
    z = _inproj(x2, w_pad)
    rqk, aqk, av = _prep(z, rot_tabs, B, S)
    ret = _retention(rqk, z, row(ret_norm_g), ret_tabs, B, S)
    att = _attention(aqk, av, att_w, B, S)
    bias_row = jnp.pad(b_if.astype(F32), (0, LANES - b_if.shape[0])).reshape(1, LANES)
    mls = _mlstm(z, conv_w.astype(F32), bias_row, row(mlstm_norm_g), B, S)
    x1 = _outproj(ret, att, mls, x2, w_out.astype(BF16), row(ln1_g), row(ln1_b))
    keys = peer_keys.reshape(PEER_HEADS * 2, PEER_NKEYS, PEER_QDIM // 2).astype(BF16)
    idx, gates = _route(x1, peer_wq.T.astype(BF16), keys)
    y = _experts(idx, x1, gates, _expert_table(peer_u, peer_v))
    return _final(x1, y, p2, ple_wg.astype(BF16), row(ple_bg), ple_wp.astype(BF16), row(ln2_g), row(ln2_b))


def kernel(x, p, w_in, conv_w, b_if, ret_norm_g, mlstm_norm_g, w_out, ln1_g, ln1_b, peer_wq, peer_keys, peer_u,
           peer_v, ple_wg, ple_bg, ple_wp, ln2_g, ln2_b):
    B, S, D = x.shape
    rot_tabs = _rotary_tables(S)
    ret_tabs = _retention_tables()
    att_w = _attention_weights()
    x2 = x.reshape(B * S, D)
    for i in range(DEPTH):
        x2 = _layer(x2, p[i].reshape(B * S, PLE_DIM), B, S, rot_tabs, ret_tabs, att_w, w_in[i], conv_w[i], b_if[i],
                    ret_norm_g[i], mlstm_norm_g[i], w_out[i], ln1_g[i], ln1_b[i], peer_wq[i], peer_keys[i],
                    peer_u[i], peer_v[i], ple_wg[i], ple_bg[i], ple_wp[i], ln2_g[i], ln2_b[i])
    return x2.reshape(B, S, D)
```

```python
import functools
import math

import jax
import jax.numpy as jnp
from jax import lax
from jax.experimental import pallas as pl
from jax.experimental.pallas import tpu as pltpu

F32 = jnp.float32
BF16 = jnp.bfloat16

D_MODEL = 1024
DEPTH = 2
RET_HEADS, RET_DK, RET_DV, RET_THETA = 4, 64, 128, 10000.0
ATT_HEADS, ATT_DIM, ROPE_THETA = 4, 64, 500000.0
ROPE_DIMS = ATT_DIM // 4
DILATED_BRANCHES = ((128, 1), (512, 4), (2048, 16))
MLSTM_HEADS, MLSTM_DIM, CONV_WIDTH = 4, 64, 4
CHUNK = 128
PEER_HEADS, PEER_NKEYS, PEER_QDIM, PEER_TOPK = 8, 128, 256, 16
PEER_EXPERTS = PEER_NKEYS * PEER_NKEYS
PLE_DIM = 256
DN_ALPHA = (2 * DEPTH) ** 0.25
LN_EPS = 1e-5

Z_WIDTH = 3584
COL_RQ, COL_RK, COL_RV, COL_RG = 0, 256, 512, 1024
COL_AQ, COL_AK, COL_AV = 1536, 1792, 2048
COL_MQ, COL_MK, COL_MV, COL_MO, COL_MG = 2304, 2560, 2816, 3072, 3328

LANES = 128
SUBLANES = 8
VMEM_LIMIT = 56 * 1024 * 1024

ATT_WINDOW = 2048
ATT_KEYS = ATT_WINDOW + CHUNK

PEER_GROUP = 8
PEER_TOKENS = 64
EXPERT_ROWS = D_MODEL // LANES

NT_DIMS = (((1,), (1,)), ((), ()))
TN_DIMS = (((0,), (0,)), ((), ()))


def _params(*sem):
    return pltpu.CompilerParams(dimension_semantics=sem, vmem_limit_bytes=VMEM_LIMIT)


def _layer_norm(r, g, b):
    mu = jnp.mean(r, axis=-1, keepdims=True)
    d = r - mu
    var = jnp.mean(d * d, axis=-1, keepdims=True)
    return d * lax.rsqrt(var + LN_EPS) * g + b


def _inproj_kernel(x_ref, w_ref, z_ref):
    xb = x_ref[...].astype(BF16)
    for j in range(Z_WIDTH // 512):
        cols = slice(j * 512, (j + 1) * 512)
        z_ref[:, cols] = jnp.dot(xb, w_ref[:, cols], preferred_element_type=F32)


def _inproj(x2, w_pad):
    T = x2.shape[0]
    tm = 256
    return pl.pallas_call(
        _inproj_kernel,
        grid=(T // tm,),
        in_specs=[pl.BlockSpec((tm, D_MODEL), lambda i: (i, 0)),
                  pl.BlockSpec((D_MODEL, Z_WIDTH), lambda i: (0, 0))],
        out_specs=pl.BlockSpec((tm, Z_WIDTH), lambda i: (i, 0)),
        out_shape=jax.ShapeDtypeStruct((T, Z_WIDTH), F32),
        compiler_params=_params("parallel"),
        name="inproj",
    )(x2, w_pad)


def _prep_kernel(zr_ref, za_ref, zv_ref, cr_ref, sr_ref, ca_ref, sa_ref, rqk_ref, aqk_ref, av_ref):
    lane = lax.broadcasted_iota(jnp.int32, zr_ref.shape, 1) % 64
    width = zr_ref.shape[1]
    x = zr_ref[...]
    xs = jnp.where(lane < RET_DK // 2, pltpu.roll(x, width - RET_DK // 2, 1), pltpu.roll(x, RET_DK // 2, 1))
    rqk_ref[...] = (x * cr_ref[...] + xs * sr_ref[...]).astype(BF16)
    x = za_ref[...]
    xs = jnp.where(lane < ROPE_DIMS // 2, pltpu.roll(x, width - ROPE_DIMS // 2, 1), pltpu.roll(x, ROPE_DIMS // 2, 1))
    aqk_ref[...] = (x * ca_ref[...] + xs * sa_ref[...]).astype(BF16)
    av_ref[...] = zv_ref[...].astype(BF16)


def _prep(z, tabs, B, S):
    T = B * S
    rb = 512
    nsb = S // rb
    row = lambda s, b: (b * nsb + s, 0)
    tab = lambda s, b: (s, 0)
    return pl.pallas_call(
        _prep_kernel,
        grid=(nsb, B),
        in_specs=[pl.BlockSpec((rb, 512), lambda s, b: (b * nsb + s, COL_RQ // 512)),
                  pl.BlockSpec((rb, 512), lambda s, b: (b * nsb + s, COL_AQ // 512)),
                  pl.BlockSpec((rb, 256), lambda s, b: (b * nsb + s, COL_AV // 256)),
                  pl.BlockSpec((rb, 512), tab), pl.BlockSpec((rb, 512), tab),
                  pl.BlockSpec((rb, 512), tab), pl.BlockSpec((rb, 512), tab)],
        out_specs=[pl.BlockSpec((rb, 512), row), pl.BlockSpec((rb, 512), row), pl.BlockSpec((rb, 256), row)],
        out_shape=[jax.ShapeDtypeStruct((T, 512), BF16), jax.ShapeDtypeStruct((T, 512), BF16),
                   jax.ShapeDtypeStruct((T, 256), BF16)],
        compiler_params=_params("parallel", "parallel"),
        name="prep",
    )(z, z, z, *tabs)


def _rotary_tables(S):
    pos = jnp.arange(S, dtype=F32)

    def table(n_rot, theta, head_dim, heads):
        half = n_rot // 2
        inv = theta ** (-jnp.arange(half, dtype=F32) / half)
        ang = pos[:, None] * inv[None, :]
        cos, sin = jnp.cos(ang), jnp.sin(ang)
        rest = head_dim - n_rot
        c = jnp.concatenate([cos, cos, jnp.ones((S, rest), F32)], -1)
        s = jnp.concatenate([-sin, sin, jnp.zeros((S, rest), F32)], -1)
        return jnp.tile(c, (1, heads)), jnp.tile(s, (1, heads))

    cr, sr = table(RET_DK, RET_THETA, RET_DK, RET_HEADS)
    ca, sa = table(ROPE_DIMS, ROPE_THETA, ATT_DIM, ATT_HEADS)
    ks = RET_DK ** -0.5
    qs = ATT_DIM ** -0.5
    return (jnp.concatenate([cr, cr * ks], -1), jnp.concatenate([sr, sr * ks], -1),
            jnp.concatenate([ca * qs, ca], -1), jnp.concatenate([sa * qs, sa], -1))


def _ret_kernel(qk_ref, v_ref, g_ref, dec_ref, xi_ref, zeta_ref, gch_ref, gain_ref, o_ref, state):
    @pl.when(pl.program_id(1) == 0)
    def _():
        state[...] = jnp.zeros_like(state)

    nchunks = qk_ref.shape[0] // CHUNK
    hq = RET_HEADS * RET_DK
    for c in range(nchunks):
        rows = slice(c * CHUNK, (c + 1) * CHUNK)
        q = qk_ref[rows, 0:hq]
        k = qk_ref[rows, hq:2 * hq]
        qx = (q.astype(F32) * xi_ref[...]).astype(BF16)
        kz = (k.astype(F32) * zeta_ref[...]).astype(BF16)
        for h in range(RET_HEADS):
            dk = slice(h * RET_DK, (h + 1) * RET_DK)
            dv = slice(h * RET_DV, (h + 1) * RET_DV)
            vh = v_ref[rows, dv].astype(BF16)
            sc = lax.dot_general(q[:, dk], k[:, dk], NT_DIMS, preferred_element_type=F32) * dec_ref[h]
            r_prev = state[h]
            out = (jnp.dot(sc.astype(BF16), vh, preferred_element_type=F32)
                   + jnp.dot(qx[:, dk], r_prev.astype(BF16), preferred_element_type=F32))
            state[h] = gch_ref[h] * r_prev + lax.dot_general(kz[:, dk], vh, TN_DIMS, preferred_element_type=F32)
            mu = jnp.mean(out, axis=-1, keepdims=True)
            d = out - mu
            var = jnp.mean(d * d, axis=-1, keepdims=True)
            y = d * lax.rsqrt(var + LN_EPS) * gain_ref[:, dv]
            gate = g_ref[rows, dv]
            o_ref[rows, dv] = y * (gate * jax.nn.sigmoid(gate))


def _retention_tables():
    H, L = RET_HEADS, CHUNK
    lg = jnp.log(1.0 - 2.0 ** (-5.0 - jnp.arange(H, dtype=F32)))
    t = jnp.arange(L, dtype=F32)
    diff = t[:, None] - t[None, :]
    decay = jnp.where(diff >= 0, jnp.exp(lg[:, None, None] * jnp.maximum(diff, 0.0)), 0.0)
    zeta = jnp.exp(lg[:, None] * (L - 1 - t)[None, :])
    xi = jnp.exp(lg[:, None] * (t + 1)[None, :])
    gch = jnp.exp(lg * L)
    spread = lambda a: jnp.repeat(a.T, RET_DK, axis=1)
    return decay, spread(xi), spread(zeta), jnp.broadcast_to(gch[:, None, None], (H, RET_DK, RET_DV))


def _retention(rqk, z, gain, tabs, B, S):
    T = B * S
    rb = 512
    nsb = S // rb
    row = lambda b, s: (b * nsb + s, 0)
    const2 = lambda b, s: (0, 0)
    const3 = lambda b, s: (0, 0, 0)
    decay, xi, zeta, gch = tabs
    return pl.pallas_call(
        _ret_kernel,
        grid=(B, nsb),
        in_specs=[pl.BlockSpec((rb, 512), row),
                  pl.BlockSpec((rb, 512), lambda b, s: (b * nsb + s, COL_RV // 512)),
                  pl.BlockSpec((rb, 512), lambda b, s: (b * nsb + s, COL_RG // 512)),
                  pl.BlockSpec(decay.shape, const3), pl.BlockSpec(xi.shape, const2),
                  pl.BlockSpec(zeta.shape, const2), pl.BlockSpec(gch.shape, const3),
                  pl.BlockSpec((1, 512), const2)],
        out_specs=pl.BlockSpec((rb, 512), row),
        out_shape=jax.ShapeDtypeStruct((T, 512), F32),
        scratch_shapes=[pltpu.VMEM((RET_HEADS, RET_DK, RET_DV), F32)],
        compiler_params=_params("parallel", "arbitrary"),
        name="retention",
    )(rqk, z, z, decay, xi, zeta, gch, gain)


def _att_kernel(q_ref, k_ref, v_ref, w_ref, o_ref, kpad, vpad):
    S = q_ref.shape[0]
    kpad[0:ATT_WINDOW, :] = jnp.zeros((ATT_WINDOW, LANES), BF16)
    vpad[0:ATT_WINDOW, :] = jnp.zeros((ATT_WINDOW, LANES), BF16)
    kpad[ATT_WINDOW:ATT_WINDOW + S, :] = k_ref[...]
    vpad[ATT_WINDOW:ATT_WINDOW + S, :] = v_ref[...]
    lane = lax.broadcasted_iota(jnp.int32, (CHUNK, LANES), 1)
    col = lax.broadcasted_iota(jnp.int32, (CHUNK, ATT_KEYS), 1)

    def body(n, carry):
        base = pl.multiple_of(n * CHUNK, CHUNK)
        q = q_ref[pl.ds(base, CHUNK), :]
        kw = kpad[pl.ds(base, ATT_KEYS), :]
        vw = vpad[pl.ds(base, ATT_KEYS), :]
        wm = jnp.where(col >= ATT_WINDOW - base, w_ref[...], 0.0)
        out = jnp.zeros((CHUNK, LANES), F32)
        for h in range(2):
            head = (lane >= h * ATT_DIM) & (lane < (h + 1) * ATT_DIM)
            qh = jnp.where(head, q, jnp.zeros_like(q))
            s = lax.dot_general(qh, kw, NT_DIMS, preferred_element_type=F32)
            s = jnp.where(wm > 0.0, s, -1e30)
            m = jnp.max(s, axis=-1, keepdims=True)
            p = jnp.exp(s - m) * wm
            den = jnp.sum(p, axis=-1, keepdims=True)
            pv = jnp.dot(p.astype(BF16), vw, preferred_element_type=F32)
            out = jnp.where(head, pv / den, out)
        o_ref[pl.ds(base, CHUNK), :] = out
        return carry

    lax.fori_loop(0, S // CHUNK, body, 0)


def _attention_weights():
    r = jnp.arange(CHUNK)[:, None]
    c = jnp.arange(ATT_KEYS)[None, :]
    delta = r + ATT_WINDOW - c
    w = jnp.zeros((CHUNK, ATT_KEYS), F32)
    for window, dil in DILATED_BRANCHES:
        w = w + ((delta >= 0) & (delta <= window) & (delta % dil == 0)).astype(F32)
    return w


def _attention(aqk, av, wtab, B, S):
    T = B * S
    npair = ATT_HEADS // 2
    return pl.pallas_call(
        _att_kernel,
        grid=(B, npair),
        in_specs=[pl.BlockSpec((S, LANES), lambda b, h: (b, h)),
                  pl.BlockSpec((S, LANES), lambda b, h: (b, npair + h)),
                  pl.BlockSpec((S, LANES), lambda b, h: (b, h)),
                  pl.BlockSpec((CHUNK, ATT_KEYS), lambda b, h: (0, 0))],
        out_specs=pl.BlockSpec((S, LANES), lambda b, h: (b, h)),
        out_shape=jax.ShapeDtypeStruct((T, ATT_HEADS * ATT_DIM), F32),
        scratch_shapes=[pltpu.VMEM((ATT_WINDOW + S, LANES), BF16), pltpu.VMEM((ATT_WINDOW + S, LANES), BF16)],
        compiler_params=_params("parallel", "parallel"),
        name="dilated_attention",
    )(aqk, aqk, av, wtab)


def _mlstm_kernel(q_ref, k_ref, v_ref, o_ref, gate_ref, convw_ref, bias_ref, gain_ref, tri_ref, out_ref,
                  c_state, n_state, m_state, prev):
    HD = MLSTM_HEADS * MLSTM_DIM
    rb = q_ref.shape[0]

    @pl.when(pl.program_id(1) == 0)
    def _():
        c_state[...] = jnp.zeros_like(c_state)
        n_state[...] = jnp.zeros_like(n_state)
        m_state[...] = jnp.zeros_like(m_state)
        prev[...] = jnp.zeros_like(prev)

    cur = jnp.concatenate([q_ref[...], k_ref[...]], axis=1)
    full = jnp.concatenate([prev[...], cur], axis=0)
    prev[...] = cur[rb - SUBLANES:rb, :]
    acc = jnp.zeros_like(cur)
    for j in range(CONV_WIDTH):
        off = SUBLANES - (CONV_WIDTH - 1) + j
        acc = acc + convw_ref[j:j + 1, :] * full[off:off + rb, :]
    qk = acc * jax.nn.sigmoid(acc)
    q_all = qk[:, 0:HD]
    k_all = qk[:, HD:2 * HD] * (MLSTM_DIM ** -0.5)

    lane = lax.broadcasted_iota(jnp.int32, (CHUNK, HD), 1)
    lane1 = lax.broadcasted_iota(jnp.int32, (1, HD), 1)
    blk_r = lax.broadcasted_iota(jnp.int32, (HD, HD), 0) // MLSTM_DIM
    blk_c = lax.broadcasted_iota(jnp.int32, (HD, HD), 1) // MLSTM_DIM
    tri = tri_ref[...] > 0.0

    for c in range(rb // CHUNK):
        rows = slice(c * CHUNK, (c + 1) * CHUNK)
        q = q_all[rows]
        k = k_all[rows]
        v = v_ref[rows, :]
        qb, kb, vb = q.astype(BF16), k.astype(BF16), v.astype(BF16)
        pre = gate_ref[rows, :] + bias_ref[...]
        lf = jnp.minimum(pre, 0.0) - jnp.log1p(jnp.exp(-jnp.abs(pre)))
        bcum = jnp.dot(tri_ref[...], lf, preferred_element_type=F32, precision=lax.Precision.HIGHEST)
        pre_t = pre.T
        bcum_t = bcum.T
        qc_all = jnp.dot(qb, c_state[...].astype(BF16), preferred_element_type=F32)
        qn = q * n_state[...]
        num = jnp.zeros((CHUNK, HD), F32)
        kw = jnp.zeros((CHUNK, HD), F32)
        dec_row = jnp.zeros((1, HD), F32)
        for h in range(MLSTM_HEADS):
            head = (lane >= h * MLSTM_DIM) & (lane < (h + 1) * MLSTM_DIM)
            head1 = (lane1 >= h * MLSTM_DIM) & (lane1 < (h + 1) * MLSTM_DIM)
            fl = MLSTM_HEADS + h
            b_col = bcum[:, fl:fl + 1]
            b_row = bcum_t[fl:fl + 1, :]
            i_col = pre[:, h:h + 1]
            i_row = pre_t[h:h + 1, :]
            m_prev = m_state[h][0:1, 0:1]
            dm = jnp.where(tri, b_col - b_row + i_row, -jnp.inf)
            inter = b_col + m_prev
            mt = jnp.maximum(inter, jnp.max(dm, axis=-1, keepdims=True))
            w_intra = jnp.exp(dm - mt)
            w_inter = jnp.exp(inter - mt)
            qh = jnp.where(head, qb, jnp.zeros_like(qb))
            sm = lax.dot_general(qh, kb, NT_DIMS, preferred_element_type=F32) * w_intra
            num_h = jnp.dot(sm.astype(BF16), vb, preferred_element_type=F32) + w_inter * qc_all
            den = (jnp.sum(sm, axis=-1, keepdims=True)
                   + w_inter * jnp.sum(jnp.where(head, qn, 0.0), axis=-1, keepdims=True))
            num = jnp.where(head, num_h / jnp.maximum(jnp.abs(den), jnp.exp(-mt)), num)
            m_new = mt[CHUNK - 1:CHUNK, :]
            b_last = b_col[CHUNK - 1:CHUNK, :]
            wk = jnp.exp(b_last - b_col + i_col - m_new)
            dec = jnp.exp(b_last + m_prev - m_new)
            kw = jnp.where(head, k * wk, kw)
            dec_row = jnp.where(head1, dec, dec_row)
            m_state[h] = jnp.broadcast_to(m_new, (SUBLANES, LANES))
        upd = lax.dot_general(kw.astype(BF16), vb, TN_DIMS, preferred_element_type=F32)
        c_state[...] = c_state[...] * dec_row + jnp.where(blk_r == blk_c, upd, 0.0)
        n_state[...] = n_state[...] * dec_row + jnp.sum(kw, axis=0, keepdims=True)
        mu = jnp.zeros((CHUNK, HD), F32)
        for h in range(MLSTM_HEADS):
            head = (lane >= h * MLSTM_DIM) & (lane < (h + 1) * MLSTM_DIM)
            mu = jnp.where(head, jnp.sum(jnp.where(head, num, 0.0), axis=-1, keepdims=True) / MLSTM_DIM, mu)
        d = num - mu
        var = jnp.zeros((CHUNK, HD), F32)
        for h in range(MLSTM_HEADS):
            head = (lane >= h * MLSTM_DIM) & (lane < (h + 1) * MLSTM_DIM)
            var = jnp.where(head, jnp.sum(jnp.where(head, d * d, 0.0), axis=-1, keepdims=True) / MLSTM_DIM, var)
        y = d * lax.rsqrt(var + LN_EPS) * gain_ref[...]
        out_ref[rows, :] = y * jax.nn.sigmoid(o_ref[rows, :])


def _mlstm(z, conv_w, bias_row, gain, B, S):
    T = B * S
    rb = 512
    nsb = S // rb
    HD = MLSTM_HEADS * MLSTM_DIM
    zcol = lambda col, w: pl.BlockSpec((rb, w), lambda b, s: (b * nsb + s, col // w))
    const2 = lambda b, s: (0, 0)
    tri = jnp.tril(jnp.ones((CHUNK, CHUNK), F32))
    return pl.pallas_call(
        _mlstm_kernel,
        grid=(B, nsb),
        in_specs=[zcol(COL_MQ, HD), zcol(COL_MK, HD), zcol(COL_MV, HD), zcol(COL_MO, HD), zcol(COL_MG, LANES),
                  pl.BlockSpec((CONV_WIDTH, 2 * HD), const2), pl.BlockSpec((1, LANES), const2),
                  pl.BlockSpec((1, HD), const2), pl.BlockSpec((CHUNK, CHUNK), const2)],
        out_specs=pl.BlockSpec((rb, HD), lambda b, s: (b * nsb + s, 0)),
        out_shape=jax.ShapeDtypeStruct((T, HD), F32),
        scratch_shapes=[pltpu.VMEM((HD, HD), F32), pltpu.VMEM((1, HD), F32),
                        pltpu.VMEM((MLSTM_HEADS, SUBLANES, LANES), F32), pltpu.VMEM((SUBLANES, 2 * HD), F32)],
        compiler_params=_params("parallel", "arbitrary"),
        name="mlstm",
    )(z, z, z, z, z, conv_w, bias_row, gain, tri)


def _outproj_kernel(ret_ref, att_ref, mls_ref, x_ref, w_ref, g_ref, b_ref, o_ref):
    acc = jnp.dot(ret_ref[...].astype(BF16), w_ref[0:512, :], preferred_element_type=F32)
    acc = acc + jnp.dot(att_ref[...].astype(BF16), w_ref[512:768, :], preferred_element_type=F32)
    acc = acc + jnp.dot(mls_ref[...].astype(BF16), w_ref[768:1024, :], preferred_element_type=F32)
    o_ref[...] = _layer_norm(DN_ALPHA * x_ref[...] + acc, g_ref[...], b_ref[...])


def _outproj(ret, att, mls, x2, w_out, g, b):
    T = x2.shape[0]
    tm = 256
    row = lambda i: (i, 0)
    const = lambda i: (0, 0)
    return pl.pallas_call(
        _outproj_kernel,
        grid=(T // tm,),
        in_specs=[pl.BlockSpec((tm, 512), row), pl.BlockSpec((tm, 256), row), pl.BlockSpec((tm, 256), row),
                  pl.BlockSpec((tm, D_MODEL), row), pl.BlockSpec((D_MODEL, D_MODEL), const),
                  pl.BlockSpec((1, D_MODEL), const), pl.BlockSpec((1, D_MODEL), const)],
        out_specs=pl.BlockSpec((tm, D_MODEL), row),
        out_shape=jax.ShapeDtypeStruct((T, D_MODEL), F32),
        compiler_params=_params("parallel"),
        name="outproj_ln",
    )(ret, att, mls, x2, w_out, g, b)


def _top16(vs, payloads=None):
    vs = list(vs)
    n = vs[0].shape[0]
    rid = lax.broadcasted_iota(jnp.int32, vs[0].shape, 0).astype(F32)
    vals = [[] for _ in vs]
    outs = [[] for _ in vs]
    for _ in range(PEER_TOPK):
        for c, v in enumerate(vs):
            m = jnp.max(v, axis=0, keepdims=True)
            am = jnp.min(jnp.where(v == m, rid, float(n)), axis=0, keepdims=True)
            sel = rid == am
            vals[c].append(m)
            if payloads is None:
                outs[c].append(am)
            else:
                outs[c].append(jnp.sum(jnp.where(sel, payloads[c], 0.0), axis=0, keepdims=True))
            vs[c] = jnp.where(sel, -jnp.inf, v)
    return [(jnp.concatenate(a, axis=0), jnp.concatenate(b, axis=0)) for a, b in zip(vals, outs)]


def _pair_tables():
    pairs = [(a, b) for a in range(PEER_TOPK) for b in range(PEER_TOPK // (a + 1))]
    rows = -(-len(pairs) // SUBLANES) * SUBLANES
    sel_a = [[1.0 if r < len(pairs) and pairs[r][0] == c else 0.0 for c in range(PEER_TOPK)] for r in range(rows)]
    sel_b = [[1.0 if r < len(pairs) and pairs[r][1] == c else 0.0 for c in range(PEER_TOPK)] for r in range(rows)]
    return jnp.asarray(sel_a, F32), jnp.asarray(sel_b, F32), len(pairs)


def _route_kernel(n_pairs, x_ref, wqt_ref, keys_ref, sela_ref, selb_ref, idx_ref, gate_ref):
    tt = x_ref.shape[0]
    xb = x_ref[...].astype(BF16)
    half = PEER_QDIM // 2
    pair_row = lax.broadcasted_iota(jnp.int32, (sela_ref.shape[0], tt), 0)
    pick = functools.partial(jnp.dot, preferred_element_type=F32, precision=lax.Precision.HIGHEST)
    idx_rows, gate_rows = [], []
    heads_per_pass = 2
    for h0 in range(0, PEER_HEADS, heads_per_pass):
        scores = []
        for hp in range(2 * h0, 2 * (h0 + heads_per_pass)):
            qt = lax.dot_general(wqt_ref[hp * half:(hp + 1) * half, :], xb, NT_DIMS, preferred_element_type=F32)
            scores.append(jnp.dot(keys_ref[hp], qt.astype(BF16), preferred_element_type=F32))
        sub = _top16(scores)
        cands, cidxs = [], []
        for h in range(heads_per_pass):
            (s1, i1), (s2, i2) = sub[2 * h], sub[2 * h + 1]
            cand = pick(sela_ref[...], s1) + pick(selb_ref[...], s2)
            cands.append(jnp.where(pair_row < n_pairs, cand, -jnp.inf))
            cidxs.append(pick(sela_ref[...], i1) * PEER_NKEYS + pick(selb_ref[...], i2))
        for top_s, top_e in _top16(cands, cidxs):
            e = jnp.exp(top_s - top_s[0:1])
            gate_rows.append(e / jnp.sum(e, axis=0, keepdims=True))
            idx_rows.append(top_e.astype(jnp.int32))
    idx_ref[...] = jnp.concatenate(idx_rows, axis=0).T
    gate_ref[...] = jnp.concatenate(gate_rows, axis=0).T


def _route(x1, wqt, keys):
    T = x1.shape[0]
    tt = 512
    kk = PEER_HEADS * PEER_TOPK
    sel_a, sel_b, n_pairs = _pair_tables()
    return pl.pallas_call(
        functools.partial(_route_kernel, n_pairs),
        grid=(T // tt,),
        in_specs=[pl.BlockSpec((tt, D_MODEL), lambda i: (i, 0)),
                  pl.BlockSpec(wqt.shape, lambda i: (0, 0)),
                  pl.BlockSpec(keys.shape, lambda i: (0, 0, 0)),
                  pl.BlockSpec(sel_a.shape, lambda i: (0, 0)),
                  pl.BlockSpec(sel_b.shape, lambda i: (0, 0))],
        out_specs=[pl.BlockSpec((tt, kk), lambda i: (i, 0)), pl.BlockSpec((tt, kk), lambda i: (i, 0))],
        out_shape=[jax.ShapeDtypeStruct((T, kk), jnp.int32), jax.ShapeDtypeStruct((T, kk), F32)],
        compiler_params=_params("parallel"),
        name="peer_route",
    )(x1, wqt, keys, sel_a, sel_b)


def _expert_table(u, v):
    ub = lax.bitcast_convert_type(u.astype(BF16), jnp.uint16).astype(jnp.uint32)
    vb = lax.bitcast_convert_type(v.astype(BF16), jnp.uint16).astype(jnp.uint32)
    return ((vb << 16) | ub).reshape(u.shape[0] * EXPERT_ROWS, LANES)


def _expert_kernel(ids_ref, x_ref, gate_ref, tab_ref, y_ref, buf, sem):
    kk = PEER_HEADS * PEER_TOPK
    n_rows = PEER_GROUP * kk
    n_groups = x_ref.shape[0] // PEER_GROUP
    step = pl.program_id(0)
    unroll = 8

    def gather_copy(src_row, slot, dst_row):
        return pltpu.make_async_copy(
            tab_ref.at[pl.ds(pl.multiple_of(src_row * EXPERT_ROWS, EXPERT_ROWS), EXPERT_ROWS), :],
            buf.at[slot, pl.ds(pl.multiple_of(dst_row * EXPERT_ROWS, EXPERT_ROWS), EXPERT_ROWS), :],
            sem.at[slot])

    def issue(g, slot):
        def body(i, carry):
            for u in range(unroll):
                j = i * unroll + u
                gather_copy(ids_ref[g * n_rows + j], slot, j).start(priority=u % 2)
            return carry
        lax.fori_loop(0, n_rows // unroll, body, 0)

    def wait(slot):
        pltpu.make_async_copy(tab_ref.at[pl.ds(0, n_rows * EXPERT_ROWS), :], buf.at[slot], sem.at[slot]).wait()

    lane = lax.broadcasted_iota(jnp.int32, (kk, LANES), 1)
    row = lax.broadcasted_iota(jnp.int32, (PEER_GROUP, LANES), 0)

    def words(slot, t, s):
        return buf[slot, pl.ds(t * kk * EXPERT_ROWS + s, kk, stride=EXPERT_ROWS), :]

    def compute(g, slot):
        base = pl.multiple_of(g * PEER_GROUP, PEER_GROUP)
        x8 = x_ref[pl.ds(base, PEER_GROUP), :]
        a_t = jnp.zeros((kk, LANES), F32)
        for t in range(PEER_GROUP):
            acc = jnp.zeros((kk, LANES), F32)
            for s in range(EXPERT_ROWS):
                wu = pltpu.bitcast(words(slot, t, s) << 16, F32)
                acc = acc + wu * x8[t:t + 1, s * LANES:(s + 1) * LANES]
            a_t = jnp.where(lane == t, jnp.sum(acc, axis=1, keepdims=True), a_t)
        a = a_t.T[0:PEER_GROUP, :]
        act = 0.5 * a * (1.0 + lax.erf(a * (1.0 / math.sqrt(2.0))))
        c = act * gate_ref[pl.ds(base, PEER_GROUP), :]
        c_t = jnp.concatenate([c, jnp.zeros((kk - PEER_GROUP, kk), F32)], axis=0).T
        y = [jnp.zeros((PEER_GROUP, LANES), F32)] * EXPERT_ROWS
        for t in range(PEER_GROUP):
            c_col = c_t[:, t:t + 1]
            for s in range(EXPERT_ROWS):
                wv = pltpu.bitcast(words(slot, t, s) & jnp.uint32(0xFFFF0000), F32)
                y[s] = jnp.where(row == t, jnp.sum(wv * c_col, axis=0, keepdims=True), y[s])
        for s in range(EXPERT_ROWS):
            y_ref[pl.ds(base, PEER_GROUP), s * LANES:(s + 1) * LANES] = y[s]

    @pl.when(step == 0)
    def _():
        issue(0, 0)

    def pair(i, carry):
        g0 = 2 * i
        issue(g0 + 1, 1)
        wait(0)
        compute(g0, 0)
        issue(g0 + 2, 0)
        wait(1)
        compute(g0 + 1, 1)
        return carry

    lax.fori_loop(0, n_groups // 2, pair, 0)

    @pl.when(step == pl.num_programs(0) - 1)
    def _():
        wait(0)


def _experts(idx, x1, gates, table):
    T = x1.shape[0]
    kk = PEER_HEADS * PEER_TOPK
    tb = PEER_TOKENS
    nsteps = T // tb
    n_rows = PEER_GROUP * kk
    assert T % tb == 0 and (tb // PEER_GROUP) % 2 == 0
    ids = idx.reshape(nsteps, tb * kk)
    ids = jnp.concatenate([ids, jnp.roll(ids[:, :n_rows], -1, axis=0)], axis=1).reshape(nsteps * (tb * kk + n_rows))
    return pl.pallas_call(
        _expert_kernel,
        grid=(nsteps,),
        in_specs=[pl.BlockSpec((tb * kk + n_rows,), lambda i: (i,), memory_space=pltpu.SMEM),
                  pl.BlockSpec((tb, D_MODEL), lambda i: (i, 0)),
                  pl.BlockSpec((tb, kk), lambda i: (i, 0)),
                  pl.BlockSpec(memory_space=pl.ANY)],
        out_specs=pl.BlockSpec((tb, D_MODEL), lambda i: (i, 0)),
        out_shape=jax.ShapeDtypeStruct((T, D_MODEL), F32),
        scratch_shapes=[pltpu.VMEM((2, n_rows * EXPERT_ROWS, LANES), jnp.uint32),
                        pltpu.SemaphoreType.DMA((2,))],
        compiler_params=_params("arbitrary"),
        name="peer_experts",
    )(ids, x1, gates, table)


def _final_kernel(x1_ref, y_ref, p_ref, wg_ref, bg_ref, wp_ref, g_ref, b_ref, o_ref):
    x1 = x1_ref[...]
    gate = jax.nn.sigmoid(jnp.dot(x1.astype(BF16), wg_ref[...], preferred_element_type=F32) + bg_ref[...])
    emb = jnp.dot(p_ref[...].astype(BF16), wp_ref[...], preferred_element_type=F32)
    o_ref[...] = _layer_norm(DN_ALPHA * x1 + y_ref[...] + gate * emb, g_ref[...], b_ref[...])


def _final(x1, y, p2, wg, bg, wp, g, b):
    T = x1.shape[0]
    tm = 256
    row = lambda i: (i, 0)
    const = lambda i: (0, 0)
    return pl.pallas_call(
        _final_kernel,
        grid=(T // tm,),
        in_specs=[pl.BlockSpec((tm, D_MODEL), row), pl.BlockSpec((tm, D_MODEL), row), pl.BlockSpec((tm, PLE_DIM), row),
                  pl.BlockSpec((D_MODEL, D_MODEL), const), pl.BlockSpec((1, D_MODEL), const),
                  pl.BlockSpec((PLE_DIM, D_MODEL), const), pl.BlockSpec((1, D_MODEL), const),
                  pl.BlockSpec((1, D_MODEL), const)],
        out_specs=pl.BlockSpec((tm, D_MODEL), row),
        out_shape=jax.ShapeDtypeStruct((T, D_MODEL), F32),
        compiler_params=_params("parallel"),
        name="ple_ln",
    )(x1, y, p2, wg, bg, wp, g, b)


def _layer(x2, p2, B, S, rot_tabs, ret_tabs, att_w, w_in, conv_w, b_if, ret_norm_g, mlstm_norm_g, w_out,
           ln1_g, ln1_b, peer_wq, peer_keys, peer_u, peer_v, ple_wg, ple_bg, ple_wp, ln2_g, ln2_b):
    row = lambda a: a.reshape(1, -1).astype(F32)
    w_pad = jnp.pad(w_in, ((0, 0), (0, Z_WIDTH - w_in.shape[1]))).astype(BF16)
    z = _inproj(x2, w_pad)
    rqk, aqk, av = _prep(z, rot_tabs, B, S)
    ret = _retention(rqk, z, row(ret_norm_g), ret_tabs, B, S)
    att = _attention(aqk, av, att_w, B, S)
    bias_row = jnp.pad(b_if.astype(F32), (0, LANES - b_if.shape[0])).reshape(1, LANES)
    mls = _mlstm(z, conv_w.astype(F32), bias_row, row(mlstm_norm_g), B, S)
    x1 = _outproj(ret, att, mls, x2, w_out.astype(BF16), row(ln1_g), row(ln1_b))
    keys = peer_keys.reshape(PEER_HEADS * 2, PEER_NKEYS, PEER_QDIM // 2).astype(BF16)
    idx, gates = _route(x1, peer_wq.T.astype(BF16), keys)
    y = _experts(idx, x1, gates, _expert_table(peer_u, peer_v))
    return _final(x1, y, p2, ple_wg.astype(BF16), row(ple_bg), ple_wp.astype(BF16), row(ln2_g), row(ln2_b))


def kernel(x, p, w_in, conv_w, b_if, ret_norm_g, mlstm_norm_g, w_out, ln1_g, ln1_b, peer_wq, peer_keys, peer_u,
           peer_v, ple_wg, ple_bg, ple_wp, ln2_g, ln2_b):
    B, S, D = x.shape
    rot_tabs = _rotary_tables(S)
    ret_tabs = _retention_tables()
    att_w = _attention_weights()
    x2 = x.reshape(B * S, D)
    for i in range(DEPTH):
        x2 = _layer(x2, p[i].reshape(B * S, PLE_DIM), B, S, rot_tabs, ret_tabs, att_w, w_in[i], conv_w[i], b_if[i],
                    ret_norm_g[i], mlstm_norm_g[i], w_out[i], ln1_g[i], ln1_b[i], peer_wq[i], peer_keys[i],
                    peer_u[i], peer_v[i], ple_wg[i], ple_bg[i], ple_wp[i], ln2_g[i], ln2_b[i])
    return x2.reshape(B, S, D)
```

```python
import functools
import math

import jax
import jax.numpy as jnp
from jax import lax
from jax.experimental import pallas as pl
from jax.experimental.pallas import tpu as pltpu

F32 = jnp.float32
BF16 = jnp.bfloat16

D_MODEL = 1024
DEPTH = 2
RET_HEADS, RET_DK, RET_DV, RET_THETA = 4, 64, 128, 10000.0
ATT_HEADS, ATT_DIM, ROPE_THETA = 4, 64, 500000.0
ROPE_DIMS = ATT_DIM // 4
DILATED_BRANCHES = ((128, 1), (512, 4), (2048, 16))
MLSTM_HEADS, MLSTM_DIM, CONV_WIDTH = 4, 64, 4
CHUNK = 128
PEER_HEADS, PEER_NKEYS, PEER_QDIM, PEER_TOPK = 8, 128, 256, 16
PEER_EXPERTS = PEER_NKEYS * PEER_NKEYS
PLE_DIM = 256
DN_ALPHA = (2 * DEPTH) ** 0.25
LN_EPS = 1e-5

Z_WIDTH = 3584
COL_RQ, COL_RK, COL_RV, COL_RG = 0, 256, 512, 1024
COL_AQ, COL_AK, COL_AV = 1536, 1792, 2048
COL_MQ, COL_MK, COL_MV, COL_MO, COL_MG = 2304, 2560, 2816, 3072, 3328

LANES = 128
SUBLANES = 8
VMEM_LIMIT = 56 * 1024 * 1024

ATT_WINDOW = 2048
ATT_KEYS = ATT_WINDOW + CHUNK

PEER_GROUP = 8
PEER_TOKENS = 64
EXPERT_ROWS = D_MODEL // LANES

NT_DIMS = (((1,), (1,)), ((), ()))
TN_DIMS = (((0,), (0,)), ((), ()))


def _params(*sem):
    return pltpu.CompilerParams(dimension_semantics=sem, vmem_limit_bytes=VMEM_LIMIT)


def _layer_norm(r, g, b):
    mu = jnp.mean(r, axis=-1, keepdims=True)
    d = r - mu
    var = jnp.mean(d * d, axis=-1, keepdims=True)
    return d * lax.rsqrt(var + LN_EPS) * g + b


def _inproj_kernel(x_ref, w_ref, z_ref):
    xb = x_ref[...].astype(BF16)
    for j in range(Z_WIDTH // 512):
        cols = slice(j * 512, (j + 1) * 512)
        z_ref[:, cols] = jnp.dot(xb, w_ref[:, cols], preferred_element_type=F32)


def _inproj(x2, w_pad):
    T = x2.shape[0]
    tm = 256
    return pl.pallas_call(
        _inproj_kernel,
        grid=(T // tm,),
        in_specs=[pl.BlockSpec((tm, D_MODEL), lambda i: (i, 0)),
                  pl.BlockSpec((D_MODEL, Z_WIDTH), lambda i: (0, 0))],
        out_specs=pl.BlockSpec((tm, Z_WIDTH), lambda i: (i, 0)),
        out_shape=jax.ShapeDtypeStruct((T, Z_WIDTH), F32),
        compiler_params=_params("parallel"),
        name="inproj",
    )(x2, w_pad)


def _prep_kernel(zr_ref, za_ref, zv_ref, cr_ref, sr_ref, ca_ref, sa_ref, rqk_ref, aqk_ref, av_ref):
    lane = lax.broadcasted_iota(jnp.int32, zr_ref.shape, 1) % 64
    width = zr_ref.shape[1]
    x = zr_ref[...]
    xs = jnp.where(lane < RET_DK // 2, pltpu.roll(x, width - RET_DK // 2, 1), pltpu.roll(x, RET_DK // 2, 1))
    rqk_ref[...] = (x * cr_ref[...] + xs * sr_ref[...]).astype(BF16)
    x = za_ref[...]
    xs = jnp.where(lane < ROPE_DIMS // 2, pltpu.roll(x, width - ROPE_DIMS // 2, 1), pltpu.roll(x, ROPE_DIMS // 2, 1))
    aqk_ref[...] = (x * ca_ref[...] + xs * sa_ref[...]).astype(BF16)
    av_ref[...] = zv_ref[...].astype(BF16)


def _prep(z, tabs, B, S):
    T = B * S
    rb = 512
    nsb = S // rb
    row = lambda s, b: (b * nsb + s, 0)
    tab = lambda s, b: (s, 0)
    return pl.pallas_call(
        _prep_kernel,
        grid=(nsb, B),
        in_specs=[pl.BlockSpec((rb, 512), lambda s, b: (b * nsb + s, COL_RQ // 512)),
                  pl.BlockSpec((rb, 512), lambda s, b: (b * nsb + s, COL_AQ // 512)),
                  pl.BlockSpec((rb, 256), lambda s, b: (b * nsb + s, COL_AV // 256)),
                  pl.BlockSpec((rb, 512), tab), pl.BlockSpec((rb, 512), tab),
                  pl.BlockSpec((rb, 512), tab), pl.BlockSpec((rb, 512), tab)],
        out_specs=[pl.BlockSpec((rb, 512), row), pl.BlockSpec((rb, 512), row), pl.BlockSpec((rb, 256), row)],
        out_shape=[jax.ShapeDtypeStruct((T, 512), BF16), jax.ShapeDtypeStruct((T, 512), BF16),
                   jax.ShapeDtypeStruct((T, 256), BF16)],
        compiler_params=_params("parallel", "parallel"),
        name="prep",
    )(z, z, z, *tabs)


def _rotary_tables(S):
    pos = jnp.arange(S, dtype=F32)

    def table(n_rot, theta, head_dim, heads):
        half = n_rot // 2
        inv = theta ** (-jnp.arange(half, dtype=F32) / half)
        ang = pos[:, None] * inv[None, :]
        cos, sin = jnp.cos(ang), jnp.sin(ang)
        rest = head_dim - n_rot
        c = jnp.concatenate([cos, cos, jnp.ones((S, rest), F32)], -1)
        s = jnp.concatenate([-sin, sin, jnp.zeros((S, rest), F32)], -1)
        return jnp.tile(c, (1, heads)), jnp.tile(s, (1, heads))

    cr, sr = table(RET_DK, RET_THETA, RET_DK, RET_HEADS)
    ca, sa = table(ROPE_DIMS, ROPE_THETA, ATT_DIM, ATT_HEADS)
    ks = RET_DK ** -0.5
    qs = ATT_DIM ** -0.5
    return (jnp.concatenate([cr, cr * ks], -1), jnp.concatenate([sr, sr * ks], -1),
            jnp.concatenate([ca * qs, ca], -1), jnp.concatenate([sa * qs, sa], -1))


def _ret_kernel(qk_ref, v_ref, g_ref, dec_ref, xi_ref, zeta_ref, gch_ref, gain_ref, o_ref, state):
    @pl.when(pl.program_id(1) == 0)
    def _():
        state[...] = jnp.zeros_like(state)

    nchunks = qk_ref.shape[0] // CHUNK
    hq = RET_HEADS * RET_DK
    for c in range(nchunks):
        rows = slice(c * CHUNK, (c + 1) * CHUNK)
        q = qk_ref[rows, 0:hq]
        k = qk_ref[rows, hq:2 * hq]
        qx = (q.astype(F32) * xi_ref[...]).astype(BF16)
        kz = (k.astype(F32) * zeta_ref[...]).astype(BF16)
        for h in range(RET_HEADS):
            dk = slice(h * RET_DK, (h + 1) * RET_DK)
            dv = slice(h * RET_DV, (h + 1) * RET_DV)
            vh = v_ref[rows, dv].astype(BF16)
            sc = lax.dot_general(q[:, dk], k[:, dk], NT_DIMS, preferred_element_type=F32) * dec_ref[h]
            r_prev = state[h]
            out = (jnp.dot(sc.astype(BF16), vh, preferred_element_type=F32)
                   + jnp.dot(qx[:, dk], r_prev.astype(BF16), preferred_element_type=F32))
            state[h] = gch_ref[h] * r_prev + lax.dot_general(kz[:, dk], vh, TN_DIMS, preferred_element_type=F32)
            mu = jnp.mean(out, axis=-1, keepdims=True)
            d = out - mu
            var = jnp.mean(d * d, axis=-1, keepdims=True)
            y = d * lax.rsqrt(var + LN_EPS) * gain_ref[:, dv]
            gate = g_ref[rows, dv]
            o_ref[rows, dv] = y * (gate * jax.nn.sigmoid(gate))


def _retention_tables():
    H, L = RET_HEADS, CHUNK
    lg = jnp.log(1.0 - 2.0 ** (-5.0 - jnp.arange(H, dtype=F32)))
    t = jnp.arange(L, dtype=F32)
    diff = t[:, None] - t[None, :]
    decay = jnp.where(diff >= 0, jnp.exp(lg[:, None, None] * jnp.maximum(diff, 0.0)), 0.0)
    zeta = jnp.exp(lg[:, None] * (L - 1 - t)[None, :])
    xi = jnp.exp(lg[:, None] * (t + 1)[None, :])
    gch = jnp.exp(lg * L)
    spread = lambda a: jnp.repeat(a.T, RET_DK, axis=1)
    return decay, spread(xi), spread(zeta), jnp.broadcast_to(gch[:, None, None], (H, RET_DK, RET_DV))


def _retention(rqk, z, gain, tabs, B, S):
    T = B * S
    rb = 512
    nsb = S // rb
    row = lambda b, s: (b * nsb + s, 0)
    const2 = lambda b, s: (0, 0)
    const3 = lambda b, s: (0, 0, 0)
    decay, xi, zeta, gch = tabs
    return pl.pallas_call(
        _ret_kernel,
        grid=(B, nsb),
        in_specs=[pl.BlockSpec((rb, 512), row),
                  pl.BlockSpec((rb, 512), lambda b, s: (b * nsb + s, COL_RV // 512)),
                  pl.BlockSpec((rb, 512), lambda b, s: (b * nsb + s, COL_RG // 512)),
                  pl.BlockSpec(decay.shape, const3), pl.BlockSpec(xi.shape, const2),
                  pl.BlockSpec(zeta.shape, const2), pl.BlockSpec(gch.shape, const3),
                  pl.BlockSpec((1, 512), const2)],
        out_specs=pl.BlockSpec((rb, 512), row),
        out_shape=jax.ShapeDtypeStruct((T, 512), F32),
        scratch_shapes=[pltpu.VMEM((RET_HEADS, RET_DK, RET_DV), F32)],
        compiler_params=_params("parallel", "arbitrary"),
        name="retention",
    )(rqk, z, z, decay, xi, zeta, gch, gain)


def _att_kernel(q_ref, k_ref, v_ref, w_ref, o_ref, kpad, vpad):
    S = q_ref.shape[0]
    kpad[0:ATT_WINDOW, :] = jnp.zeros((ATT_WINDOW, LANES), BF16)
    vpad[0:ATT_WINDOW, :] = jnp.zeros((ATT_WINDOW, LANES), BF16)
    kpad[ATT_WINDOW:ATT_WINDOW + S, :] = k_ref[...]
    vpad[ATT_WINDOW:ATT_WINDOW + S, :] = v_ref[...]
    lane = lax.broadcasted_iota(jnp.int32, (CHUNK, LANES), 1)
    col = lax.broadcasted_iota(jnp.int32, (CHUNK, ATT_KEYS), 1)

    def body(n, carry):
        base = pl.multiple_of(n * CHUNK, CHUNK)
        q = q_ref[pl.ds(base, CHUNK), :]
        kw = kpad[pl.ds(base, ATT_KEYS), :]
        vw = vpad[pl.ds(base, ATT_KEYS), :]
        wm = jnp.where(col >= ATT_WINDOW - base, w_ref[...], 0.0)
        out = jnp.zeros((CHUNK, LANES), F32)
        for h in range(2):
            head = (lane >= h * ATT_DIM) & (lane < (h + 1) * ATT_DIM)
            qh = jnp.where(head, q, jnp.zeros_like(q))
            s = lax.dot_general(qh, kw, NT_DIMS, preferred_element_type=F32)
            s = jnp.where(wm > 0.0, s, -1e30)
            m = jnp.max(s, axis=-1, keepdims=True)
            p = jnp.exp(s - m) * wm
            den = jnp.sum(p, axis=-1, keepdims=True)
            pv = jnp.dot(p.astype(BF16), vw, preferred_element_type=F32)
            out = jnp.where(head, pv / den, out)
        o_ref[pl.ds(base, CHUNK), :] = out
        return carry

    lax.fori_loop(0, S // CHUNK, body, 0)


def _attention_weights():
    r = jnp.arange(CHUNK)[:, None]
    c = jnp.arange(ATT_KEYS)[None, :]
    delta = r + ATT_WINDOW - c
    w = jnp.zeros((CHUNK, ATT_KEYS), F32)
    for window, dil in DILATED_BRANCHES:
        w = w + ((delta >= 0) & (delta <= window) & (delta % dil == 0)).astype(F32)
    return w


def _attention(aqk, av, wtab, B, S):
    T = B * S
    npair = ATT_HEADS // 2
    return pl.pallas_call(
        _att_kernel,
        grid=(B, npair),
        in_specs=[pl.BlockSpec((S, LANES), lambda b, h: (b, h)),
                  pl.BlockSpec((S, LANES), lambda b, h: (b, npair + h)),
                  pl.BlockSpec((S, LANES), lambda b, h: (b, h)),
                  pl.BlockSpec((CHUNK, ATT_KEYS), lambda b, h: (0, 0))],
        out_specs=pl.BlockSpec((S, LANES), lambda b, h: (b, h)),
        out_shape=jax.ShapeDtypeStruct((T, ATT_HEADS * ATT_DIM), F32),
        scratch_shapes=[pltpu.VMEM((ATT_WINDOW + S, LANES), BF16), pltpu.VMEM((ATT_WINDOW + S, LANES), BF16)],
        compiler_params=_params("parallel", "parallel"),
        name="dilated_attention",
    )(aqk, aqk, av, wtab)


def _mlstm_kernel(q_ref, k_ref, v_ref, o_ref, gate_ref, convw_ref, bias_ref, gain_ref, tri_ref, out_ref,
                  c_state, n_state, m_state, prev):
    HD = MLSTM_HEADS * MLSTM_DIM
    rb = q_ref.shape[0]

    @pl.when(pl.program_id(1) == 0)
    def _():
        c_state[...] = jnp.zeros_like(c_state)
        n_state[...] = jnp.zeros_like(n_state)
        m_state[...] = jnp.zeros_like(m_state)
        prev[...] = jnp.zeros_like(prev)

    cur = jnp.concatenate([q_ref[...], k_ref[...]], axis=1)
    full = jnp.concatenate([prev[...], cur], axis=0)
    prev[...] = cur[rb - SUBLANES:rb, :]
    acc = jnp.zeros_like(cur)
    for j in range(CONV_WIDTH):
        off = SUBLANES - (CONV_WIDTH - 1) + j
        acc = acc + convw_ref[j:j + 1, :] * full[off:off + rb, :]
    qk = acc * jax.nn.sigmoid(acc)
    q_all = qk[:, 0:HD]
    k_all = qk[:, HD:2 * HD] * (MLSTM_DIM ** -0.5)

    lane = lax.broadcasted_iota(jnp.int32, (CHUNK, HD), 1)
    lane1 = lax.broadcasted_iota(jnp.int32, (1, HD), 1)
    blk_r = lax.broadcasted_iota(jnp.int32, (HD, HD), 0) // MLSTM_DIM
    blk_c = lax.broadcasted_iota(jnp.int32, (HD, HD), 1) // MLSTM_DIM
    tri = tri_ref[...] > 0.0

    for c in range(rb // CHUNK):
        rows = slice(c * CHUNK, (c + 1) * CHUNK)
        q = q_all[rows]
        k = k_all[rows]
        v = v_ref[rows, :]
        qb, kb, vb = q.astype(BF16), k.astype(BF16), v.astype(BF16)
        pre = gate_ref[rows, :] + bias_ref[...]
        lf = jnp.minimum(pre, 0.0) - jnp.log1p(jnp.exp(-jnp.abs(pre)))
        bcum = jnp.dot(tri_ref[...], lf, preferred_element_type=F32, precision=lax.Precision.HIGHEST)
        pre_t = pre.T
        bcum_t = bcum.T
        qc_all = jnp.dot(qb, c_state[...].astype(BF16), preferred_element_type=F32)
        qn = q * n_state[...]
        num = jnp.zeros((CHUNK, HD), F32)
        kw = jnp.zeros((CHUNK, HD), F32)
        dec_row = jnp.zeros((1, HD), F32)
        for h in range(MLSTM_HEADS):
            head = (lane >= h * MLSTM_DIM) & (lane < (h + 1) * MLSTM_DIM)
            head1 = (lane1 >= h * MLSTM_DIM) & (lane1 < (h + 1) * MLSTM_DIM)
            fl = MLSTM_HEADS + h
            b_col = bcum[:, fl:fl + 1]
            b_row = bcum_t[fl:fl + 1, :]
            i_col = pre[:, h:h + 1]
            i_row = pre_t[h:h + 1, :]
            m_prev = m_state[h][0:1, 0:1]
            dm = jnp.where(tri, b_col - b_row + i_row, -jnp.inf)
            inter = b_col + m_prev
            mt = jnp.maximum(inter, jnp.max(dm, axis=-1, keepdims=True))
            w_intra = jnp.exp(dm - mt)
            w_inter = jnp.exp(inter - mt)
            qh = jnp.where(head, qb, jnp.zeros_like(qb))
            sm = lax.dot_general(qh, kb, NT_DIMS, preferred_element_type=F32) * w_intra
            num_h = jnp.dot(sm.astype(BF16), vb, preferred_element_type=F32) + w_inter * qc_all
            den = (jnp.sum(sm, axis=-1, keepdims=True)
                   + w_inter * jnp.sum(jnp.where(head, qn, 0.0), axis=-1, keepdims=True))
            num = jnp.where(head, num_h / jnp.maximum(jnp.abs(den), jnp.exp(-mt)), num)
            m_new = mt[CHUNK - 1:CHUNK, :]
            b_last = b_col[CHUNK - 1:CHUNK, :]
            wk = jnp.exp(b_last - b_col + i_col - m_new)
            dec = jnp.exp(b_last + m_prev - m_new)
            kw = jnp.where(head, k * wk, kw)
            dec_row = jnp.where(head1, dec, dec_row)
            m_state[h] = jnp.broadcast_to(m_new, (SUBLANES, LANES))
        upd = lax.dot_general(kw.astype(BF16), vb, TN_DIMS, preferred_element_type=F32)
        c_state[...] = c_state[...] * dec_row + jnp.where(blk_r == blk_c, upd, 0.0)
        n_state[...] = n_state[...] * dec_row + jnp.sum(kw, axis=0, keepdims=True)
        mu = jnp.zeros((CHUNK, HD), F32)
        for h in range(MLSTM_HEADS):
            head = (lane >= h * MLSTM_DIM) & (lane < (h + 1) * MLSTM_DIM)
            mu = jnp.where(head, jnp.sum(jnp.where(head, num, 0.0), axis=-1, keepdims=True) / MLSTM_DIM, mu)
        d = num - mu
        var = jnp.zeros((CHUNK, HD), F32)
        for h in range(MLSTM_HEADS):
            head = (lane >= h * MLSTM_DIM) & (lane < (h + 1) * MLSTM_DIM)
            var = jnp.where(head, jnp.sum(jnp.where(head, d * d, 0.0), axis=-1, keepdims=True) / MLSTM_DIM, var)
        y = d * lax.rsqrt(var + LN_EPS) * gain_ref[...]
        out_ref[rows, :] = y * jax.nn.sigmoid(o_ref[rows, :])


def _mlstm(z, conv_w, bias_row, gain, B, S):
    T = B * S
    rb = 512
    nsb = S // rb
    HD = MLSTM_HEADS * MLSTM_DIM
    zcol = lambda col, w: pl.BlockSpec((rb, w), lambda b, s: (b * nsb + s, col // w))
    const2 = lambda b, s: (0, 0)
    tri = jnp.tril(jnp.ones((CHUNK, CHUNK), F32))
    return pl.pallas_call(
        _mlstm_kernel,
        grid=(B, nsb),
        in_specs=[zcol(COL_MQ, HD), zcol(COL_MK, HD), zcol(COL_MV, HD), zcol(COL_MO, HD), zcol(COL_MG, LANES),
                  pl.BlockSpec((CONV_WIDTH, 2 * HD), const2), pl.BlockSpec((1, LANES), const2),
                  pl.BlockSpec((1, HD), const2), pl.BlockSpec((CHUNK, CHUNK), const2)],
        out_specs=pl.BlockSpec((rb, HD), lambda b, s: (b * nsb + s, 0)),
        out_shape=jax.ShapeDtypeStruct((T, HD), F32),
        scratch_shapes=[pltpu.VMEM((HD, HD), F32), pltpu.VMEM((1, HD), F32),
                        pltpu.VMEM((MLSTM_HEADS, SUBLANES, LANES), F32), pltpu.VMEM((SUBLANES, 2 * HD), F32)],
        compiler_params=_params("parallel", "arbitrary"),
        name="mlstm",
    )(z, z, z, z, z, conv_w, bias_row, gain, tri)


def _outproj_kernel(ret_ref, att_ref, mls_ref, x_ref, w_ref, g_ref, b_ref, o_ref):
    acc = jnp.dot(ret_ref[...].astype(BF16), w_ref[0:512, :], preferred_element_type=F32)
    acc = acc + jnp.dot(att_ref[...].astype(BF16), w_ref[512:768, :], preferred_element_type=F32)
    acc = acc + jnp.dot(mls_ref[...].astype(BF16), w_ref[768:1024, :], preferred_element_type=F32)
    o_ref[...] = _layer_norm(DN_ALPHA * x_ref[...] + acc, g_ref[...], b_ref[...])


def _outproj(ret, att, mls, x2, w_out, g, b):
    T = x2.shape[0]
    tm = 256
    row = lambda i: (i, 0)
    const = lambda i: (0, 0)
    return pl.pallas_call(
        _outproj_kernel,
        grid=(T // tm,),
        in_specs=[pl.BlockSpec((tm, 512), row), pl.BlockSpec((tm, 256), row), pl.BlockSpec((tm, 256), row),
                  pl.BlockSpec((tm, D_MODEL), row), pl.BlockSpec((D_MODEL, D_MODEL), const),
                  pl.BlockSpec((1, D_MODEL), const), pl.BlockSpec((1, D_MODEL), const)],
        out_specs=pl.BlockSpec((tm, D_MODEL), row),
        out_shape=jax.ShapeDtypeStruct((T, D_MODEL), F32),
        compiler_params=_params("parallel"),
        name="outproj_ln",
    )(ret, att, mls, x2, w_out, g, b)


def _top16(vs, payloads=None):
    vs = list(vs)
    n = vs[0].shape[0]
    rid = lax.broadcasted_iota(jnp.int32, vs[0].shape, 0).astype(F32)
    vals = [[] for _ in vs]
    outs = [[] for _ in vs]
    for _ in range(PEER_TOPK):
        for c, v in enumerate(vs):
            m = jnp.max(v, axis=0, keepdims=True)
            am = jnp.min(jnp.where(v == m, rid, float(n)), axis=0, keepdims=True)
            sel = rid == am
            vals[c].append(m)
            if payloads is None:
                outs[c].append(am)
            else:
                outs[c].append(jnp.sum(jnp.where(sel, payloads[c], 0.0), axis=0, keepdims=True))
            vs[c] = jnp.where(sel, -jnp.inf, v)
    return [(jnp.concatenate(a, axis=0), jnp.concatenate(b, axis=0)) for a, b in zip(vals, outs)]


def _pair_tables():
    pairs = [(a, b) for a in range(PEER_TOPK) for b in range(PEER_TOPK // (a + 1))]
    rows = -(-len(pairs) // SUBLANES) * SUBLANES
    sel_a = [[1.0 if r < len(pairs) and pairs[r][0] == c else 0.0 for c in range(PEER_TOPK)] for r in range(rows)]
    sel_b = [[1.0 if r < len(pairs) and pairs[r][1] == c else 0.0 for c in range(PEER_TOPK)] for r in range(rows)]
    return jnp.asarray(sel_a, F32), jnp.asarray(sel_b, F32), len(pairs)


def _route_kernel(n_pairs, x_ref, wqt_ref, keys_ref, sela_ref, selb_ref, idx_ref, gate_ref):
    tt = x_ref.shape[0]
    xb = x_ref[...].astype(BF16)
    half = PEER_QDIM // 2
    pair_row = lax.broadcasted_iota(jnp.int32, (sela_ref.shape[0], tt), 0)
    pick = functools.partial(jnp.dot, preferred_element_type=F32, precision=lax.Precision.HIGHEST)
    idx_rows, gate_rows = [], []
    heads_per_pass = 2
    for h0 in range(0, PEER_HEADS, heads_per_pass):
        scores = []
        for hp in range(2 * h0, 2 * (h0 + heads_per_pass)):
            qt = lax.dot_general(wqt_ref[hp * half:(hp + 1) * half, :], xb, NT_DIMS, preferred_element_type=F32)
            scores.append(jnp.dot(keys_ref[hp], qt.astype(BF16), preferred_element_type=F32))
        sub = _top16(scores)
        cands, cidxs = [], []
        for h in range(heads_per_pass):
            (s1, i1), (s2, i2) = sub[2 * h], sub[2 * h + 1]
            cand = pick(sela_ref[...], s1) + pick(selb_ref[...], s2)
            cands.append(jnp.where(pair_row < n_pairs, cand, -jnp.inf))
            cidxs.append(pick(sela_ref[...], i1) * PEER_NKEYS + pick(selb_ref[...], i2))
        for top_s, top_e in _top16(cands, cidxs):
            e = jnp.exp(top_s - top_s[0:1])
            gate_rows.append(e / jnp.sum(e, axis=0, keepdims=True))
            idx_rows.append(top_e.astype(jnp.int32))
    idx_ref[...] = jnp.concatenate(idx_rows, axis=0).T
    gate_ref[...] = jnp.concatenate(gate_rows, axis=0).T


def _route(x1, wqt, keys):
    T = x1.shape[0]
    tt = 512
    kk = PEER_HEADS * PEER_TOPK
    sel_a, sel_b, n_pairs = _pair_tables()
    return pl.pallas_call(
        functools.partial(_route_kernel, n_pairs),
        grid=(T // tt,),
        in_specs=[pl.BlockSpec((tt, D_MODEL), lambda i: (i, 0)),
                  pl.BlockSpec(wqt.shape, lambda i: (0, 0)),
                  pl.BlockSpec(keys.shape, lambda i: (0, 0, 0)),
                  pl.BlockSpec(sel_a.shape, lambda i: (0, 0)),
                  pl.BlockSpec(sel_b.shape, lambda i: (0, 0))],
        out_specs=[pl.BlockSpec((tt, kk), lambda i: (i, 0)), pl.BlockSpec((tt, kk), lambda i: (i, 0))],
        out_shape=[jax.ShapeDtypeStruct((T, kk), jnp.int32), jax.ShapeDtypeStruct((T, kk), F32)],
        compiler_params=_params("parallel"),
        name="peer_route",
    )(x1, wqt, keys, sel_a, sel_b)


def _expert_table(u, v):
    ub = lax.bitcast_convert_type(u.astype(BF16), jnp.uint16).astype(jnp.uint32)
    vb = lax.bitcast_convert_type(v.astype(BF16), jnp.uint16).astype(jnp.uint32)
    return ((vb << 16) | ub).reshape(u.shape[0] * EXPERT_ROWS, LANES)


def _expert_kernel(ids_ref, x_ref, gate_ref, tab_ref, y_ref, buf, sem):
    kk = PEER_HEADS * PEER_TOPK
    n_rows = PEER_GROUP * kk
    n_groups = x_ref.shape[0] // PEER_GROUP
    step = pl.program_id(0)
    unroll = 8

    def gather_copy(src_row, slot, dst_row):
        return pltpu.make_async_copy(
            tab_ref.at[src_row],
            buf.at[slot, :, dst_row, :],
            sem.at[slot])

    def issue(g, slot):
        def body(i, carry):
            for u in range(unroll):
                j = i * unroll + u
                gather_copy(ids_ref[g * n_rows + j], slot, j).start(priority=u % 2)
            return carry
        lax.fori_loop(0, n_rows // unroll, body, 0)

    def wait(slot):
        pltpu.make_async_copy(buf.at[slot], buf.at[slot], sem.at[slot]).wait()

    lane = lax.broadcasted_iota(jnp.int32, (kk, LANES), 1)
    row = lax.broadcasted_iota(jnp.int32, (PEER_GROUP, LANES), 0)

    def words(slot, t, s):
        return buf[slot, s, pl.ds(t * kk, kk), :]

    def compute(g, slot):
        base = pl.multiple_of(g * PEER_GROUP, PEER_GROUP)
        x8 = x_ref[pl.ds(base, PEER_GROUP), :]
        a_t = jnp.zeros((kk, LANES), F32)
        for t in range(PEER_GROUP):
            acc = jnp.zeros((kk, LANES), F32)
            for s in range(EXPERT_ROWS):
                wu = pltpu.bitcast(words(slot, t, s) << 16, F32)
                acc = acc + wu * x8[t:t + 1, s * LANES:(s + 1) * LANES]
            a_t = jnp.where(lane == t, jnp.sum(acc, axis=1, keepdims=True), a_t)
        a = a_t.T[0:PEER_GROUP, :]
        act = 0.5 * a * (1.0 + lax.erf(a * (1.0 / math.sqrt(2.0))))
        c = act * gate_ref[pl.ds(base, PEER_GROUP), :]
        c_t = jnp.concatenate([c, jnp.zeros((kk - PEER_GROUP, kk), F32)], axis=0).T
        y = [jnp.zeros((PEER_GROUP, LANES), F32)] * EXPERT_ROWS
        for t in range(PEER_GROUP):
            c_col = c_t[:, t:t + 1]
            for s in range(EXPERT_ROWS):
                wv = pltpu.bitcast(words(slot, t, s) & jnp.uint32(0xFFFF0000), F32)
                y[s] = jnp.where(row == t, jnp.sum(wv * c_col, axis=0, keepdims=True), y[s])
        for s in range(EXPERT_ROWS):
            y_ref[pl.ds(base, PEER_GROUP), s * LANES:(s + 1) * LANES] = y[s]

    @pl.when(step == 0)
    def _():
        issue(0, 0)

    def pair(i, carry):
        g0 = 2 * i
        issue(g0 + 1, 1)
        wait(0)
        compute(g0, 0)
        issue(g0 + 2, 0)
        wait(1)
        compute(g0 + 1, 1)
        return carry

    lax.fori_loop(0, n_groups // 2, pair, 0)

    @pl.when(step == pl.num_programs(0) - 1)
    def _():
        wait(0)


def _experts(idx, x1, gates, table):
    T = x1.shape[0]
    kk = PEER_HEADS * PEER_TOPK
    tb = PEER_TOKENS
    nsteps = T // tb
    n_rows = PEER_GROUP * kk
    assert T % tb == 0 and (tb // PEER_GROUP) % 2 == 0
    ids = idx.reshape(nsteps, tb * kk)
    ids = jnp.concatenate([ids, jnp.roll(ids[:, :n_rows], -1, axis=0)], axis=1).reshape(nsteps * (tb * kk + n_rows))
    return pl.pallas_call(
        _expert_kernel,
        grid=(nsteps,),
        in_specs=[pl.BlockSpec((tb * kk + n_rows,), lambda i: (i,), memory_space=pltpu.SMEM),
                  pl.BlockSpec((tb, D_MODEL), lambda i: (i, 0)),
                  pl.BlockSpec((tb, kk), lambda i: (i, 0)),
                  pl.BlockSpec(memory_space=pl.ANY)],
        out_specs=pl.BlockSpec((tb, D_MODEL), lambda i: (i, 0)),
        out_shape=jax.ShapeDtypeStruct((T, D_MODEL), F32),
        scratch_shapes=[pltpu.VMEM((2, EXPERT_ROWS, n_rows, LANES), jnp.uint32),
                        pltpu.SemaphoreType.DMA((2,))],
        compiler_params=_params("arbitrary"),
        name="peer_experts",
    )(ids, x1, gates, table.reshape(-1, EXPERT_ROWS, LANES))


def _final_kernel(x1_ref, y_ref, p_ref, wg_ref, bg_ref, wp_ref, g_ref, b_ref, o_ref):
    x1 = x1_ref[...]
    gate = jax.nn.sigmoid(jnp.dot(x1.astype(BF16), wg_ref[...], preferred_element_type=F32) + bg_ref[...])
    emb = jnp.dot(p_ref[...].astype(BF16), wp_ref[...], preferred_element_type=F32)
    o_ref[...] = _layer_norm(DN_ALPHA * x1 + y_ref[...] + gate * emb, g_ref[...], b_ref[...])


def _final(x1, y, p2, wg, bg, wp, g, b):
    T = x1.shape[0]
    tm = 256
    row = lambda i: (i, 0)
    const = lambda i: (0, 0)
    return pl.pallas_call(
        _final_kernel,
        grid=(T // tm,),
        in_specs=[pl.BlockSpec((tm, D_MODEL), row), pl.BlockSpec((tm, D_MODEL), row), pl.BlockSpec((tm, PLE_DIM), row),
                  pl.BlockSpec((D_MODEL, D_MODEL), const), pl.BlockSpec((1, D_MODEL), const),
                  pl.BlockSpec((PLE_DIM, D_MODEL), const), pl.BlockSpec((1, D_MODEL), const),
                  pl.BlockSpec((1, D_MODEL), const)],
        out_specs=pl.BlockSpec((tm, D_MODEL), row),
        out_shape=jax.ShapeDtypeStruct((T, D_MODEL), F32),
        compiler_params=_params("parallel"),
        name="ple_ln",
    )(x1, y, p2, wg, bg, wp, g, b)


def _layer(x2, p2, B, S, rot_tabs, ret_tabs, att_w, w_in, conv_w, b_if, ret_norm_g, mlstm_norm_g, w_out,
           ln1_g, ln1_b, peer_wq, peer_keys, peer_u, peer_v, ple_wg, ple_bg, ple_wp, ln2_g, ln2_b):
    row = lambda a: a.reshape(1, -1).astype(F32)
    w_pad = jnp.pad(w_in, ((0, 0), (0, Z_WIDTH - w_in.shape[1]))).astype(BF16)
    z = _inproj(x2, w_pad)
    rqk, aqk, av = _prep(z, rot_tabs, B, S)
    ret = _retention(rqk, z, row(ret_norm_g), ret_tabs, B, S)
    att = _attention(aqk, av, att_w, B, S)
    bias_row = jnp.pad(b_if.astype(F32), (0, LANES - b_if.shape[0])).reshape(1, LANES)
    mls = _mlstm(z, conv_w.astype(F32), bias_row, row(mlstm_norm_g), B, S)
    x1 = _outproj(ret, att, mls, x2, w_out.astype(BF16), row(ln1_g), row(ln1_b))
    keys = peer_keys.reshape(PEER_HEADS * 2, PEER_NKEYS, PEER_QDIM // 2).astype(BF16)
    idx, gates = _route(x1, peer_wq.T.astype(BF16), keys)
    y = _experts(idx, x1, gates, _expert_table(peer_u, peer_v))
    return _final(x1, y, p2, ple_wg.astype(BF16), row(ple_bg), ple_wp.astype(BF16), row(ln2_g), row(ln2_b))


def kernel(x, p, w_in, conv_w, b_if, ret_norm_g, mlstm_norm_g, w_out, ln1_g, ln1_b, peer_wq, peer_keys, peer_u,
           peer_v, ple_wg, ple_bg, ple_wp, ln2_g, ln2_b):
    B, S, D = x.shape
    rot_tabs = _rotary_tables(S)
    ret_tabs = _retention_tables()
    att_w = _attention_weights()
    x2 = x.reshape(B * S, D)
    for i in range(DEPTH):
        x2 = _layer(x2, p[i].reshape(B * S, PLE_DIM), B, S, rot_tabs, ret_tabs, att_w, w_in[i], conv_w[i], b_if[i],
                    ret_norm_g[i], mlstm_norm_g[i], w_out[i], ln1_g[i], ln1_b[i], peer_wq[i], peer_keys[i],
                    peer_u[i], peer_v[i], ple_wg[i], ple_bg[i], ple_wp[i], ln2_g[i], ln2_b[i])
    return x2.reshape(B, S, D)
```
